```python
import math
import jax, jax.numpy as jnp
from jax import lax
import numpy as np

D_MODEL = 2048
BATCH = 4
SEQ = 4096
DEPTH = 4

GRID_W = 64
ROPE_THETA = 10000.0
Q_BLOCK = 128
LN_EPS = 1e-5
RMS_EPS = 1e-6

DEEPNORM_ALPHA = (2.0 * DEPTH) ** 0.25
DEEPNORM_BETA = (8.0 * DEPTH) ** -0.25

MLA_HEADS = 8
MLA_Q_RANK = 512
MLA_KV_RANK = 256
MLA_NOPE_DIM = 128
MLA_ROPE_DIM = 64
MLA_V_DIM = 128

GLA_HEADS = 4
GLA_DK = 128
GLA_DV = 256
GLA_GATE_RANK = 16
GLA_GATE_TAU = 16.0
GLA_CHUNK = 64

IN_SIZES = (
    MLA_Q_RANK,
    MLA_KV_RANK,
    MLA_ROPE_DIM,
    GLA_HEADS * GLA_DK,
    GLA_HEADS * GLA_DK,
    GLA_HEADS * GLA_DV,
    GLA_HEADS * GLA_DV,
    2 * GLA_GATE_RANK,
)
IN_WIDTH = sum(IN_SIZES)
MIX_WIDTH = MLA_HEADS * MLA_V_DIM + GLA_HEADS * GLA_DV

GQA_HEADS = 16
GQA_KV_HEADS = 4
GQA_HEAD_DIM = 128

N_EXPERTS = 16
EXPERT_FF = 1536
CAPACITY_FACTOR = 2

kernel_name = "hybrid_mla_gla_gqa_ecmoe_deepnorm"


def rms_norm(x, g):
    xf = x.astype(jnp.float32)
    y = xf * lax.rsqrt(jnp.mean(xf * xf, axis=-1, keepdims=True) + RMS_EPS)
    return (y * g.astype(jnp.float32)).astype(x.dtype)


def layer_norm(x, g, b):
    xf = x.astype(jnp.float32)
    mu = jnp.mean(xf, axis=-1, keepdims=True)
    var = jnp.mean(jnp.square(xf - mu), axis=-1, keepdims=True)
    y = (xf - mu) * lax.rsqrt(var + LN_EPS) * g.astype(jnp.float32) + b.astype(jnp.float32)
    return y.astype(x.dtype)


def axial_rope_tables(seq_len, rot_dim):
    rows = seq_len // GRID_W
    row = jnp.repeat(jnp.arange(rows, dtype=jnp.int32), GRID_W).astype(jnp.float32)
    col = jnp.tile(jnp.arange(GRID_W, dtype=jnp.int32), rows).astype(jnp.float32)
    half = rot_dim // 2
    inv_freq = ROPE_THETA ** (-jnp.arange(0, half, 2, dtype=jnp.float32) / half)
    ang = jnp.concatenate([row[:, None] * inv_freq, col[:, None] * inv_freq], axis=-1)
    return jnp.cos(ang), jnp.sin(ang)


def apply_rope(x, cos, sin):
    B, S, H, R = x.shape
    xf = x.astype(jnp.float32).reshape(B, S, H, R // 2, 2)
    c, s = cos[None, :, None, :], sin[None, :, None, :]
    x1, x2 = xf[..., 0], xf[..., 1]
    out = jnp.stack([x1 * c - x2 * s, x1 * s + x2 * c], axis=-1)
    return out.reshape(B, S, H, R).astype(x.dtype)


def block_attention(q, k, v, scale):
    B, S, KVH, G, Dq = q.shape
    Dv = v.shape[-1]
    nb = S // Q_BLOCK
    qb = q.reshape(B, nb, Q_BLOCK, KVH, G, Dq).transpose(1, 0, 2, 3, 4, 5)

    def one_block(q_blk):
        s = jnp.einsum('bqhgd,bkhd->bhgqk', q_blk, k,
                       preferred_element_type=jnp.float32) * scale
        p = jax.nn.softmax(s, axis=-1)
        return jnp.einsum('bhgqk,bkhd->bqhgd', p.astype(v.dtype), v)

    o = lax.map(one_block, qb)
    return o.transpose(1, 0, 2, 3, 4, 5).reshape(B, S, KVH, G, Dv)


def gla_chunked(q, k, v, log_a):
    B, S, H, Dk = q.shape
    Dv = v.shape[-1]
    C = GLA_CHUNK
    N = S // C
    q = q.reshape(B, N, C, H, Dk)
    k = k.reshape(B, N, C, H, Dk)
    v = v.reshape(B, N, C, H, Dv)
    b = jnp.cumsum(log_a.reshape(B, N, C, H, Dk), axis=2)
    b_last = b[:, :, -1]
    q_in = q * jnp.exp(b)
    k_in = k * jnp.exp(-b)
    k_st = k * jnp.exp(b_last[:, :, None] - b)
    mask = jnp.tril(jnp.ones((C, C), dtype=bool))
    att = jnp.einsum('bnihd,bnjhd->bnhij', q_in, k_in)
    att = jnp.where(mask, att, 0.0)
    o_intra = jnp.einsum('bnhij,bnjhv->bnihv', att, v)
    dS = jnp.einsum('bnjhd,bnjhv->bnhdv', k_st, v)
    decay = jnp.exp(b_last)

    def step(s_prev, inp):
        ds_n, dec_n = inp
        return dec_n[..., None] * s_prev + ds_n, s_prev

    s0 = jnp.zeros((B, H, Dk, Dv), jnp.float32)
    _, s_before = lax.scan(step, s0, (dS.transpose(1, 0, 2, 3, 4), decay.transpose(1, 0, 2, 3)))
    s_before = s_before.transpose(1, 0, 2, 3, 4)
    o_inter = jnp.einsum('bnihd,bnhdv->bnihv', q_in, s_before)
    return (o_intra + o_inter).reshape(B, S, H, Dv)


def mla_gla_mixer(x, w_in, q_norm, w_uq, kv_norm, w_ukv, gate_w2, gate_b, gla_norm, w_out,
                  cos_r, sin_r):
    B, S, _ = x.shape
    h = x @ w_in
    offsets = np.cumsum(IN_SIZES)[:-1].tolist()
    c_q, c_kv, k_rope, g_q, g_k, g_v, g_r, g_lat = jnp.split(h, offsets, axis=-1)

    q = (rms_norm(c_q, q_norm) @ w_uq).reshape(B, S, MLA_HEADS, MLA_NOPE_DIM + MLA_ROPE_DIM)
    q_nope, q_pe = q[..., :MLA_NOPE_DIM], q[..., MLA_NOPE_DIM:]
    q_pe = apply_rope(q_pe, cos_r, sin_r)
    kv = (rms_norm(c_kv, kv_norm) @ w_ukv).reshape(B, S, MLA_HEADS, MLA_NOPE_DIM + MLA_V_DIM)
    k_nope, v_mla = kv[..., :MLA_NOPE_DIM], kv[..., MLA_NOPE_DIM:]
    k_pe = apply_rope(k_rope[:, :, None, :], cos_r, sin_r)
    k_pe = jnp.broadcast_to(k_pe, (B, S, MLA_HEADS, MLA_ROPE_DIM))
    q_full = jnp.concatenate([q_nope, q_pe], axis=-1)[:, :, :, None, :]
    k_full = jnp.concatenate([k_nope, k_pe], axis=-1)
    o_mla = block_attention(q_full, k_full, v_mla, (MLA_NOPE_DIM + MLA_ROPE_DIM) ** -0.5)
    o_mla = o_mla.reshape(B, S, MLA_HEADS * MLA_V_DIM)

    f32 = jnp.float32
    gq = g_q.astype(f32).reshape(B, S, GLA_HEADS, GLA_DK) * (GLA_DK ** -0.5)
    gk = g_k.astype(f32).reshape(B, S, GLA_HEADS, GLA_DK)
    gv = g_v.astype(f32).reshape(B, S, GLA_HEADS, GLA_DV)
    lat_f, lat_b = g_lat[..., :GLA_GATE_RANK], g_lat[..., GLA_GATE_RANK:]
    la_f = jax.nn.log_sigmoid((lat_f @ gate_w2[0] + gate_b[0]).astype(f32)) / GLA_GATE_TAU
    la_b = jax.nn.log_sigmoid((lat_b @ gate_w2[1] + gate_b[1]).astype(f32)) / GLA_GATE_TAU
    la_f = la_f.reshape(B, S, GLA_HEADS, GLA_DK)
    la_b = la_b.reshape(B, S, GLA_HEADS, GLA_DK)
    o_fwd = gla_chunked(gq, gk, gv, la_f)
    flip = lambda a: jnp.flip(a, axis=1)
    o_bwd = flip(gla_chunked(flip(gq), flip(gk), flip(gv), flip(la_b)))
    o_gla = rms_norm(o_fwd + o_bwd, gla_norm.reshape(GLA_HEADS, GLA_DV))
    o_gla = o_gla.reshape(B, S, GLA_HEADS * GLA_DV).astype(x.dtype) * jax.nn.silu(g_r)

    return jnp.concatenate([o_mla, o_gla], axis=-1) @ w_out


def gqa_axial_mixer(x, w_qkv, q_norm, k_norm, w_out, cos_g, sin_g):
    B, S, _ = x.shape
    h = x @ w_qkv
    nq = GQA_HEADS * GQA_HEAD_DIM
    nk = GQA_KV_HEADS * GQA_HEAD_DIM
    q = h[..., :nq].reshape(B, S, GQA_HEADS, GQA_HEAD_DIM)
    k = h[..., nq:nq + nk].reshape(B, S, GQA_KV_HEADS, GQA_HEAD_DIM)
    v = h[..., nq + nk:].reshape(B, S, GQA_KV_HEADS, GQA_HEAD_DIM)
    q = apply_rope(rms_norm(q, q_norm), cos_g, sin_g)
    k = apply_rope(rms_norm(k, k_norm), cos_g, sin_g)
    q = q.reshape(B, S, GQA_KV_HEADS, GQA_HEADS // GQA_KV_HEADS, GQA_HEAD_DIM)
    o = block_attention(q, k, v, GQA_HEAD_DIM ** -0.5).reshape(B, S, nq)
    return o @ w_out


def expert_choice_moe(x, w_router, w1, w3, w2):
    B, T, D = x.shape
    cap = CAPACITY_FACTOR * T // N_EXPERTS
    logits = jnp.einsum('btd,de->bte', x, w_router, preferred_element_type=jnp.float32)
    aff = jax.nn.softmax(logits, axis=-1)
    gate, idx = lax.top_k(aff.transpose(0, 2, 1), cap)
    xs = jax.vmap(lambda xb, ib: xb[ib])(x, idx)
    hdn = jax.nn.silu(jnp.einsum('becd,edf->becf', xs, w1)) * jnp.einsum('becd,edf->becf', xs, w3)
    y = jnp.einsum('becf,efd->becd', hdn, w2) * gate[..., None].astype(x.dtype)
    return jax.vmap(
        lambda yb, ib: jnp.zeros((T, D), yb.dtype).at[ib.reshape(-1)].add(yb.reshape(-1, D))
    )(y, idx)


def setup_inputs(seed: int = 0) -> dict:
    key = jax.random.key(seed)
    ks = jax.random.split(key, 24)
    n_even = (DEPTH + 1) // 2
    n_odd = DEPTH // 2
    D = D_MODEL
    nrm = lambda k, shape, fan_in, s=1.0: jax.random.normal(k, shape, jnp.float32) * (s * fan_in ** -0.5)
    gain = lambda k, shape: 1.0 + 0.02 * jax.random.normal(k, shape, jnp.float32)
    small = lambda k, shape: 0.01 * jax.random.normal(k, shape, jnp.float32)
    return {
        "x": jax.random.normal(ks[0], (BATCH, SEQ, D), jnp.float32),
        "mix_w_in": nrm(ks[1], (n_even, D, IN_WIDTH), D),
        "mla_q_norm": gain(ks[2], (n_even, MLA_Q_RANK)),
        "mla_w_uq": nrm(ks[3], (n_even, MLA_Q_RANK, MLA_HEADS * (MLA_NOPE_DIM + MLA_ROPE_DIM)), MLA_Q_RANK),
        "mla_kv_norm": gain(ks[4], (n_even, MLA_KV_RANK)),
        "mla_w_ukv": nrm(ks[5], (n_even, MLA_KV_RANK, MLA_HEADS * (MLA_NOPE_DIM + MLA_V_DIM)), MLA_KV_RANK),
        "gla_gate_w2": nrm(ks[6], (n_even, 2, GLA_GATE_RANK, GLA_HEADS * GLA_DK), GLA_GATE_RANK),
        "gla_gate_b": small(ks[7], (n_even, 2, GLA_HEADS * GLA_DK)),
        "gla_out_norm": gain(ks[8], (n_even, GLA_HEADS * GLA_DV)),
        "mix_w_out": nrm(ks[9], (n_even, MIX_WIDTH, D), MIX_WIDTH, DEEPNORM_BETA),
        "gqa_w_qkv": nrm(ks[10], (n_odd, D, (GQA_HEADS + 2 * GQA_KV_HEADS) * GQA_HEAD_DIM), D),
        "gqa_q_norm": gain(ks[11], (n_odd, GQA_HEAD_DIM)),
        "gqa_k_norm": gain(ks[12], (n_odd, GQA_HEAD_DIM)),
        "gqa_w_out": nrm(ks[13], (n_odd, GQA_HEADS * GQA_HEAD_DIM, D), GQA_HEADS * GQA_HEAD_DIM, DEEPNORM_BETA),
        "moe_router": nrm(ks[14], (DEPTH, D, N_EXPERTS), D),
        "moe_w1": nrm(ks[15], (DEPTH, N_EXPERTS, D, EXPERT_FF), D),
        "moe_w3": nrm(ks[16], (DEPTH, N_EXPERTS, D, EXPERT_FF), D),
        "moe_w2": nrm(ks[17], (DEPTH, N_EXPERTS, EXPERT_FF, D), EXPERT_FF, DEEPNORM_BETA),
        "ln_mix_g": gain(ks[18], (DEPTH, D)),
        "ln_mix_b": small(ks[19], (DEPTH, D)),
        "ln_ffn_g": gain(ks[20], (DEPTH, D)),
        "ln_ffn_b": small(ks[21], (DEPTH, D)),
    }


def reference(x, mix_w_in, mla_q_norm, mla_w_uq, mla_kv_norm, mla_w_ukv, gla_gate_w2, gla_gate_b,
              gla_out_norm, mix_w_out, gqa_w_qkv, gqa_q_norm, gqa_k_norm, gqa_w_out,
              moe_router, moe_w1, moe_w3, moe_w2, ln_mix_g, ln_mix_b, ln_ffn_g, ln_ffn_b):
    S = x.shape[1]
    cos_r, sin_r = axial_rope_tables(S, MLA_ROPE_DIM)
    cos_g, sin_g = axial_rope_tables(S, GQA_HEAD_DIM)
    for layer in range(DEPTH):
        i = layer // 2
        if layer % 2 == 0:
            m = mla_gla_mixer(x, mix_w_in[i], mla_q_norm[i], mla_w_uq[i], mla_kv_norm[i],
                              mla_w_ukv[i], gla_gate_w2[i], gla_gate_b[i], gla_out_norm[i],
                              mix_w_out[i], cos_r, sin_r)
        else:
            m = gqa_axial_mixer(x, gqa_w_qkv[i], gqa_q_norm[i], gqa_k_norm[i], gqa_w_out[i],
                                cos_g, sin_g)
        x = layer_norm(DEEPNORM_ALPHA * x + m, ln_mix_g[layer], ln_mix_b[layer])
        f = expert_choice_moe(x, moe_router[layer], moe_w1[layer], moe_w3[layer], moe_w2[layer])
        x = layer_norm(DEEPNORM_ALPHA * x + f, ln_ffn_g[layer], ln_ffn_b[layer])
    return x
```

```python
import functools

import jax
import jax.numpy as jnp
from jax import lax
from jax.experimental import pallas as pl
from jax.experimental.pallas import tpu as pltpu

F32 = jnp.float32
BF16 = jnp.bfloat16

GRID_W = 64
ROPE_THETA = 10000.0
LN_EPS = 1e-5
RMS_EPS = 1e-6
DEPTH = 4
DEEPNORM_ALPHA = (2.0 * DEPTH) ** 0.25

MLA_HEADS = 8
MLA_Q_RANK = 512
MLA_KV_RANK = 256
MLA_NOPE_DIM = 128
MLA_ROPE_DIM = 64
MLA_V_DIM = 128
MLA_QK_PAD = 256

GLA_HEADS = 4
GLA_DK = 128
GLA_DV = 256
GLA_GATE_RANK = 16
GLA_GATE_TAU = 16.0
GLA_CHUNK = 64
GLA_GROUP = 256

GQA_HEADS = 16
GQA_KV_HEADS = 4
GQA_HEAD_DIM = 128

N_EXPERTS = 16
CAPACITY_FACTOR = 2

LANES = 128
VMEM_LIMIT_BYTES = 56 * 1024 * 1024

H_CQ = 0
H_GQ = 512
H_GK = 1024
H_CKV = 1536
H_TAIL = 1792
H_GV = 2048
H_GR = 3072
H_WIDTH = 4096


def _cparams(sem):
    return pltpu.CompilerParams(dimension_semantics=sem, vmem_limit_bytes=VMEM_LIMIT_BYTES)


def _nt_dot(a, b):
    return lax.dot_general(a, b, (((1,), (1,)), ((), ())), preferred_element_type=F32)


def _tn_dot(a, b):
    return lax.dot_general(a, b, (((0,), (0,)), ((), ())), preferred_element_type=F32)


def _rope(x, cos, sin_signed):
    lane = lax.broadcasted_iota(jnp.int32, x.shape, 1)
    partner = jnp.where((lane & 1) == 0, pltpu.roll(x, LANES - 1, 1), pltpu.roll(x, 1, 1))
    return x * cos + partner * sin_signed


def _rms(x, gain):
    return x * lax.rsqrt(jnp.mean(x * x, axis=-1, keepdims=True) + RMS_EPS) * gain


def _layer_norm(z, g, b):
    mu = jnp.mean(z, axis=-1, keepdims=True)
    zc = z - mu
    var = jnp.mean(zc * zc, axis=-1, keepdims=True)
    return zc * lax.rsqrt(var + LN_EPS) * g + b


def _mm_kernel(x_ref, w_ref, o_ref):
    o_ref[...] = jnp.dot(x_ref[...], w_ref[...], preferred_element_type=F32).astype(o_ref.dtype)


def _matmul(x, w, out_dtype, tm, tn):
    m, k = x.shape
    n = w.shape[1]
    tm, tn = min(tm, m), min(tn, n)
    return pl.pallas_call(
        _mm_kernel,
        grid=(n // tn, m // tm),
        in_specs=[pl.BlockSpec((tm, k), lambda j, i: (i, 0)),
                  pl.BlockSpec((k, tn), lambda j, i: (0, j))],
        out_specs=pl.BlockSpec((tm, tn), lambda j, i: (i, j)),
        out_shape=jax.ShapeDtypeStruct((m, n), out_dtype),
        compiler_params=_cparams(("parallel", "parallel")),
        name="dense_matmul",
    )(x, w)


def _qkv_kernel(x_ref, w_ref, g_ref, cos_ref, sin_ref, o_ref, *, n_normed_tiles, heads_per_tile):
    acc = jnp.dot(x_ref[...], w_ref[...], preferred_element_type=F32)
    j = pl.program_id(0)

    @pl.when(j < n_normed_tiles)
    def _():
        cos = cos_ref[...]
        sin = sin_ref[...]
        for hd in range(heads_per_tile):
            sl = slice(hd * GQA_HEAD_DIM, (hd + 1) * GQA_HEAD_DIM)
            y = _rms(acc[:, sl], g_ref[:, sl])
            o_ref[:, sl] = _rope(y, cos, sin).astype(o_ref.dtype)

    @pl.when(j >= n_normed_tiles)
    def _():
        o_ref[...] = acc.astype(o_ref.dtype)


def _qkv_proj(xb, w, gains, cos, sin, seq, tm):
    m, k = xb.shape
    n = w.shape[1]
    tn = 4 * GQA_HEAD_DIM
    tm = min(tm, seq)
    n_tiles = n // tn
    pos_blocks = seq // tm
    kern = functools.partial(_qkv_kernel, n_normed_tiles=n_tiles - 1, heads_per_tile=4)
    return pl.pallas_call(
        kern,
        grid=(n_tiles, m // tm),
        in_specs=[pl.BlockSpec((tm, k), lambda j, i: (i, 0)),
                  pl.BlockSpec((k, tn), lambda j, i: (0, j)),
                  pl.BlockSpec((None, 1, tn), lambda j, i: (j, 0, 0)),
                  pl.BlockSpec((tm, LANES), lambda j, i: (i % pos_blocks, 0)),
                  pl.BlockSpec((tm, LANES), lambda j, i: (i % pos_blocks, 0))],
        out_specs=pl.BlockSpec((tm, tn), lambda j, i: (i, j)),
        out_shape=jax.ShapeDtypeStruct((m, n), BF16),
        compiler_params=_cparams(("parallel", "parallel")),
        name="gqa_qkv_proj",
    )(xb, w, gains, cos, sin)


def _attn_kernel(q_ref, k_ref, v_ref, o_ref, *, group, dqk, dv):
    k = k_ref[...]
    v = v_ref[...]
    for g in range(group):
        q = q_ref[:, g * dqk:(g + 1) * dqk]
        s = _nt_dot(q, k)
        m = jnp.max(s, axis=1, keepdims=True)
        p = jnp.exp(s - m)
        l = jnp.sum(p, axis=1, keepdims=True)
        o = jnp.dot(p.astype(BF16), v, preferred_element_type=F32)
        o_ref[:, g * dv:(g + 1) * dv] = (o / l).astype(o_ref.dtype)


def _attention(q_arr, k_arr, v_arr, *, batch, seq, kv_heads, group, dqk, dv,
               q_off, k_off, v_off, tq):
    tq = min(tq, seq)
    nq = seq // tq
    kern = functools.partial(_attn_kernel, group=group, dqk=dqk, dv=dv)
    return pl.pallas_call(
        kern,
        grid=(batch, kv_heads, nq),
        in_specs=[pl.BlockSpec((tq, group * dqk), lambda b, h, i: (b * nq + i, q_off + h)),
                  pl.BlockSpec((seq, dqk), lambda b, h, i: (b, k_off + h)),
                  pl.BlockSpec((seq, dv), lambda b, h, i: (b, v_off + h))],
        out_specs=pl.BlockSpec((tq, group * dv), lambda b, h, i: (b * nq + i, h)),
        out_shape=jax.ShapeDtypeStruct((batch * seq, kv_heads * group * dv), BF16),
        compiler_params=_cparams(("parallel", "parallel", "parallel")),
        name="softmax_attention",
    )(q_arr, k_arr, v_arr)


def _mla_prep_kernel(cq_ref, ckv_ref, tail_ref, qn_ref, wuq_ref, kvn_ref, wukv_ref, wg_ref, gb_ref,
                     cos_ref, sin_ref, q_ref, k_ref, v_ref, la_ref):
    cos = cos_ref[...]
    sin = sin_ref[...]
    scale = (MLA_NOPE_DIM + MLA_ROPE_DIM) ** -0.5
    q = jnp.dot(_rms(cq_ref[...], qn_ref[...]).astype(BF16), wuq_ref[...],
                preferred_element_type=F32) * scale
    kv = jnp.dot(_rms(ckv_ref[...], kvn_ref[...]).astype(BF16), wukv_ref[...],
                 preferred_element_type=F32)
    tail = tail_ref[...]
    lane = lax.broadcasted_iota(jnp.int32, tail.shape, 1)
    k_pe = jnp.where(lane < MLA_ROPE_DIM, _rope(tail, cos, sin), 0.0).astype(BF16)
    for hd in range(MLA_HEADS):
        c0 = hd * MLA_QK_PAD
        q_ref[:, c0:c0 + LANES] = q[:, c0:c0 + LANES].astype(BF16)
        q_ref[:, c0 + LANES:c0 + 2 * LANES] = _rope(q[:, c0 + LANES:c0 + 2 * LANES], cos, sin).astype(BF16)
        k_ref[:, c0:c0 + LANES] = kv[:, hd * LANES:(hd + 1) * LANES].astype(BF16)
        k_ref[:, c0 + LANES:c0 + 2 * LANES] = k_pe
    nv = MLA_HEADS * MLA_V_DIM
    v_ref[...] = kv[:, nv:].astype(BF16)
    gate = jnp.dot(tail.astype(BF16), wg_ref[...], preferred_element_type=F32) + gb_ref[...]
    log_sig = jnp.minimum(gate, 0.0) - jnp.log1p(jnp.exp(-jnp.abs(gate)))
    la_ref[...] = log_sig / GLA_GATE_TAU


def _mla_prep(h, qn, wuq, kvn, wukv, wg, gb, cos, sin, seq, tm):
    m = h.shape[0]
    tm = min(tm, seq)
    pos_blocks = seq // tm
    qw = MLA_HEADS * MLA_QK_PAD
    vw = MLA_HEADS * MLA_V_DIM
    gw = 2 * GLA_HEADS * GLA_DK
    const = lambda i: (0, 0)
    return pl.pallas_call(
        _mla_prep_kernel,
        grid=(m // tm,),
        in_specs=[pl.BlockSpec((tm, MLA_Q_RANK), lambda i: (i, H_CQ // MLA_Q_RANK)),
                  pl.BlockSpec((tm, MLA_KV_RANK), lambda i: (i, H_CKV // MLA_KV_RANK)),
                  pl.BlockSpec((tm, LANES), lambda i: (i, H_TAIL // LANES)),
                  pl.BlockSpec((1, MLA_Q_RANK), const),
                  pl.BlockSpec((MLA_Q_RANK, qw), const),
                  pl.BlockSpec((1, MLA_KV_RANK), const),
                  pl.BlockSpec((MLA_KV_RANK, 2 * vw), const),
                  pl.BlockSpec((LANES, gw), const),
                  pl.BlockSpec((1, gw), const),
                  pl.BlockSpec((tm, LANES), lambda i: (i % pos_blocks, 0)),
                  pl.BlockSpec((tm, LANES), lambda i: (i % pos_blocks, 0))],
        out_specs=[pl.BlockSpec((tm, qw), lambda i: (i, 0)),
                   pl.BlockSpec((tm, qw), lambda i: (i, 0)),
                   pl.BlockSpec((tm, vw), lambda i: (i, 0)),
                   pl.BlockSpec((tm, gw), lambda i: (i, 0))],
        out_shape=[jax.ShapeDtypeStruct((m, qw), BF16),
                   jax.ShapeDtypeStruct((m, qw), BF16),
                   jax.ShapeDtypeStruct((m, vw), BF16),
                   jax.ShapeDtypeStruct((m, gw), F32)],
        compiler_params=_cparams(("parallel",)),
        name="mla_prep",
    )(h, h, h, qn, wuq, kvn, wukv, wg, gb, cos, sin)


def _split3(x):
    hi = x.astype(BF16)
    r1 = x - hi.astype(F32)
    mid = r1.astype(BF16)
    lo = (r1 - mid.astype(F32)).astype(BF16)
    return hi, mid, lo


def _gla_kernel(q_ref, k_ref, v_ref, gr_ref, laf_ref, lab_ref, gn_ref, o_ref,
                of_s, ob_s, qf_s, qb_s, kf_s, kb_s, df_s, db_s, stf_s, stb_s):
    seq = q_ref.shape[0]
    c = GLA_CHUNK
    r = min(GLA_GROUP, seq)
    cpg = r // c
    n_groups = seq // r
    n_chunks = seq // c
    scale = GLA_DK ** -0.5

    row = lax.broadcasted_iota(jnp.int32, (r, r), 0)
    col = lax.broadcasted_iota(jnp.int32, (r, r), 1)
    same = (row // c) == (col // c)
    tril = same & (col <= row)
    triu = same & (col >= row)
    tril_b = tril.astype(F32).astype(BF16)
    triu_b = triu.astype(F32).astype(BF16)

    def group_body(gi, carry):
        r0 = pl.multiple_of(gi * r, r)
        rows = pl.ds(r0, r)
        q = q_ref[rows, :] * scale
        k = k_ref[rows, :]
        v = v_ref[rows, :].astype(BF16)
        for la_ref, mask, mask_b, edge, o_s, q_s, k_s, d_s in (
                (laf_ref, tril, tril_b, c - 1, of_s, qf_s, kf_s, df_s),
                (lab_ref, triu, triu_b, 0, ob_s, qb_s, kb_s, db_s)):
            hi, mid, lo = _split3(la_ref[rows, :])
            b = (jnp.dot(mask_b, hi, preferred_element_type=F32)
                 + jnp.dot(mask_b, mid, preferred_element_type=F32)
                 + jnp.dot(mask_b, lo, preferred_element_type=F32))
            b_edge = jnp.concatenate(
                [jnp.broadcast_to(b[ci * c + edge:ci * c + edge + 1, :], (c, GLA_DK)) for ci in range(cpg)],
                axis=0)
            q_in = (q * jnp.exp(b)).astype(BF16)
            k_in = (k * jnp.exp(-b)).astype(BF16)
            k_st = (k * jnp.exp(b_edge - b)).astype(BF16)
            att = jnp.where(mask, _nt_dot(q_in, k_in), 0.0)
            o_s[rows, :] = jnp.dot(att.astype(BF16), v, preferred_element_type=F32)
            q_s[rows, :] = q_in
            k_s[rows, :] = k_st
            for ci in range(cpg):
                d_s[pl.ds(gi * cpg + ci, 1), :] = jnp.exp(b[ci * c + edge:ci * c + edge + 1, :])
        return carry

    lax.fori_loop(0, n_groups, group_body, 0)

    stf_s[...] = jnp.zeros_like(stf_s)
    stb_s[...] = jnp.zeros_like(stb_s)

    def chunk_body(i, carry):
        for n, o_s, q_s, k_s, d_s, st_s in ((i, of_s, qf_s, kf_s, df_s, stf_s),
                                            (n_chunks - 1 - i, ob_s, qb_s, kb_s, db_s, stb_s)):
            rows = pl.ds(pl.multiple_of(n * c, c), c)
            st = st_s[...]
            o_s[rows, :] += _nt_dot(q_s[rows, :], st.astype(BF16))
            ds = _tn_dot(v_ref[rows, :].astype(BF16), k_s[rows, :])
            st_s[...] = d_s[pl.ds(n, 1), :] * st + ds
        return carry

    lax.fori_loop(0, n_chunks, chunk_body, 0)

    gain = gn_ref[...]

    def out_body(gi, carry):
        rows = pl.ds(pl.multiple_of(gi * r, r), r)
        o = _rms(of_s[rows, :] + ob_s[rows, :], gain)
        gr = gr_ref[rows, :]
        o_ref[rows, :] = (o * (gr * jax.nn.sigmoid(gr))).astype(o_ref.dtype)
        return carry

    lax.fori_loop(0, n_groups, out_body, 0)


def _gla(h, la, gn, batch, seq):
    dk, dv, nh = GLA_DK, GLA_DV, GLA_HEADS
    n_chunks = seq // GLA_CHUNK
    return pl.pallas_call(
        _gla_kernel,
        grid=(batch, nh),
        in_specs=[pl.BlockSpec((seq, dk), lambda b, hd: (b, H_GQ // dk + hd)),
                  pl.BlockSpec((seq, dk), lambda b, hd: (b, H_GK // dk + hd)),
                  pl.BlockSpec((seq, dv), lambda b, hd: (b, H_GV // dv + hd)),
                  pl.BlockSpec((seq, dv), lambda b, hd: (b, H_GR // dv + hd)),
                  pl.BlockSpec((seq, dk), lambda b, hd: (b, hd)),
                  pl.BlockSpec((seq, dk), lambda b, hd: (b, nh + hd)),
                  pl.BlockSpec((None, 1, dv), lambda b, hd: (hd, 0, 0))],
        out_specs=pl.BlockSpec((seq, dv), lambda b, hd: (b, hd)),
        out_shape=jax.ShapeDtypeStruct((batch * seq, nh * dv), BF16),
        scratch_shapes=[pltpu.VMEM((seq, dv), F32), pltpu.VMEM((seq, dv), F32),
                        pltpu.VMEM((seq, dk), BF16), pltpu.VMEM((seq, dk), BF16),
                        pltpu.VMEM((seq, dk), BF16), pltpu.VMEM((seq, dk), BF16),
                        pltpu.VMEM((n_chunks, dk), F32), pltpu.VMEM((n_chunks, dk), F32),
                        pltpu.VMEM((dv, dk), F32), pltpu.VMEM((dv, dk), F32)],
        compiler_params=_cparams(("arbitrary", "arbitrary")),
        name="gla_bidirectional",
    )(h, h, h, h, la, la, gn)


def _proj_ln_kernel(*refs, n_in):
    a_refs = refs[:n_in]
    w_refs = refs[n_in:2 * n_in]
    x_ref, g_ref, b_ref, wrh_ref, wrl_ref, xo_ref, xb_ref, lg_ref = refs[2 * n_in:]
    mix = jnp.dot(a_refs[0][...], w_refs[0][...], preferred_element_type=F32)
    for a_ref, w_ref in zip(a_refs[1:], w_refs[1:]):
        mix = mix + jnp.dot(a_ref[...], w_ref[...], preferred_element_type=F32)
    y = _layer_norm(DEEPNORM_ALPHA * x_ref[...] + mix, g_ref[...], b_ref[...])
    xo_ref[...] = y
    y_hi = y.astype(BF16)
    xb_ref[...] = y_hi
    y_lo = (y - y_hi.astype(F32)).astype(BF16)
    wrh = wrh_ref[...]
    lg_ref[...] = _nt_dot(wrh, y_hi) + _nt_dot(wrh, y_lo) + _nt_dot(wrl_ref[...], y_hi)


def _proj_ln(acts, weights, x, g, b, wr_hi, wr_lo, tm):
    m, d = x.shape
    tm = min(tm, m)
    n_in = len(acts)
    n_e = wr_hi.shape[0]
    const = lambda i: (0, 0)
    in_specs = ([pl.BlockSpec((tm, a.shape[1]), lambda i: (i, 0)) for a in acts]
                + [pl.BlockSpec(w.shape, const) for w in weights]
                + [pl.BlockSpec((tm, d), lambda i: (i, 0)),
                   pl.BlockSpec((1, d), const), pl.BlockSpec((1, d), const),
                   pl.BlockSpec((n_e, d), const), pl.BlockSpec((n_e, d), const)])
    return pl.pallas_call(
        functools.partial(_proj_ln_kernel, n_in=n_in),
        grid=(m // tm,),
        in_specs=in_specs,
        out_specs=[pl.BlockSpec((tm, d), lambda i: (i, 0)),
                   pl.BlockSpec((tm, d), lambda i: (i, 0)),
                   pl.BlockSpec((n_e, tm), lambda i: (0, i))],
        out_shape=[jax.ShapeDtypeStruct((m, d), F32),
                   jax.ShapeDtypeStruct((m, d), BF16),
                   jax.ShapeDtypeStruct((n_e, m), F32)],
        compiler_params=_cparams(("parallel",)),
        name="out_proj_layernorm_router",
    )(*acts, *weights, x, g, b, wr_hi, wr_lo)


def _ln_kernel(z_ref, g_ref, b_ref, xo_ref, xb_ref):
    y = _layer_norm(z_ref[...], g_ref[...], b_ref[...])
    xo_ref[...] = y
    xb_ref[...] = y.astype(BF16)


def _ln(z, g, b, tm):
    m, d = z.shape
    tm = min(tm, m)
    const = lambda i: (0, 0)
    return pl.pallas_call(
        _ln_kernel,
        grid=(m // tm,),
        in_specs=[pl.BlockSpec((tm, d), lambda i: (i, 0)),
                  pl.BlockSpec((1, d), const), pl.BlockSpec((1, d), const)],
        out_specs=[pl.BlockSpec((tm, d), lambda i: (i, 0)),
                   pl.BlockSpec((tm, d), lambda i: (i, 0))],
        out_shape=[jax.ShapeDtypeStruct((m, d), F32), jax.ShapeDtypeStruct((m, d), BF16)],
        compiler_params=_cparams(("parallel",)),
        name="layernorm",
    )(z, g, b)


def _lane_cumsum(m01, out_s):
    n = m01.shape[1]
    r = lax.broadcasted_iota(jnp.int32, (LANES, LANES), 0)
    c = lax.broadcasted_iota(jnp.int32, (LANES, LANES), 1)
    upper = (r <= c).astype(F32).astype(BF16)
    off = jnp.zeros((m01.shape[0], 1), F32)
    for j in range(n // LANES):
        blk = m01[:, j * LANES:(j + 1) * LANES].astype(BF16)
        cs = jnp.dot(blk, upper, preferred_element_type=F32) + off
        out_s[:, j * LANES:(j + 1) * LANES] = cs
        off = cs[:, LANES - 1:LANES]


def _route_kernel(lg_ref, idx_ref, gate_ref, aff_s, cum_s, key_s, *, cap, slot_block):
    lg = lg_ref[...]
    ex = jnp.exp(lg - jnp.max(lg, axis=0, keepdims=True))
    aff = ex / jnp.sum(ex, axis=0, keepdims=True)
    aff_s[...] = aff
    bits = lax.bitcast_convert_type(aff, jnp.int32)
    n_e = lg.shape[0]
    capf = float(cap)

    def bisect(_, carry):
        lo, hi = carry
        mid = lo + ((hi - lo + 1) >> 1)
        cnt = jnp.sum(jnp.where(bits >= mid, 1.0, 0.0), axis=1, keepdims=True)
        ok = cnt >= capf
        return jnp.where(ok, mid, lo), jnp.where(ok, hi, mid - 1)

    lo0 = jnp.zeros((n_e, 1), jnp.int32)
    hi0 = jnp.full((n_e, 1), 0x7F800000, jnp.int32)
    thr, _ = lax.fori_loop(0, 32, bisect, (lo0, hi0))

    gt = bits > thr
    eq = bits == thr
    need = capf - jnp.sum(jnp.where(gt, 1.0, 0.0), axis=1, keepdims=True)
    eq01 = jnp.where(eq, 1.0, 0.0)
    _lane_cumsum(eq01, cum_s)
    tie_rank = cum_s[...] - eq01
    sel = gt | (eq & (tie_rank < need))
    sel01 = jnp.where(sel, 1.0, 0.0)
    _lane_cumsum(sel01, cum_s)
    key_s[...] = jnp.where(sel, cum_s[...], 0.0)

    nsb = cap // slot_block

    def extract(it, carry):
        e = it // nsb
        sb = it % nsb
        cum = cum_s[pl.ds(e, 1), :]
        key = key_s[pl.ds(e, 1), :]
        a = aff_s[pl.ds(e, 1), :]
        slot = (lax.broadcasted_iota(jnp.int32, (slot_block, 1), 0) + sb * slot_block).astype(F32)
        tok = jnp.sum(jnp.where(cum <= slot, 1.0, 0.0), axis=1, keepdims=True)
        gate = jnp.sum(jnp.where(key == slot + 1.0, a, 0.0), axis=1, keepdims=True)
        rows = pl.ds(pl.multiple_of(it * slot_block, slot_block), slot_block)
        idx_ref[rows, :] = tok.astype(jnp.int32)
        gate_ref[rows, :] = gate
        return carry

    lax.fori_loop(0, n_e * nsb, extract, 0)


def _route(logits_t, batch, seq, cap):
    n_e = logits_t.shape[0]
    slot_block = min(16, cap)
    kern = functools.partial(_route_kernel, cap=cap, slot_block=slot_block)
    return pl.pallas_call(
        kern,
        grid=(batch,),
        in_specs=[pl.BlockSpec((n_e, seq), lambda b: (0, b))],
        out_specs=[pl.BlockSpec((n_e * cap, 1), lambda b: (b, 0)),
                   pl.BlockSpec((n_e * cap, 1), lambda b: (b, 0))],
        out_shape=[jax.ShapeDtypeStruct((batch * n_e * cap, 1), jnp.int32),
                   jax.ShapeDtypeStruct((batch * n_e * cap, 1), F32)],
        scratch_shapes=[pltpu.VMEM((n_e, seq), F32), pltpu.VMEM((n_e, seq), F32),
                        pltpu.VMEM((n_e, seq), F32)],
        compiler_params=_cparams(("parallel",)),
        name="expert_choice_route",
    )(logits_t)


def _gather_kernel(idx_ref, x_hbm, o_ref, land, sem, *, cap, seq, batch):
    b = pl.program_id(0) % batch

    def row_copy(s, tok):
        return pltpu.make_async_copy(x_hbm.at[pl.ds(b * seq + tok, 1), :], land.at[pl.ds(s, 1), :], sem)

    def issue(s, carry):
        row_copy(s, idx_ref[0, 0, s]).start()
        return carry

    lax.fori_loop(0, cap, issue, 0)

    def drain(s, carry):
        row_copy(s, 0).wait()
        return carry

    lax.fori_loop(0, cap, drain, 0)
    o_ref[...] = land[...].astype(o_ref.dtype)


def _gather(idx_eb, x, batch, seq, cap):
    d = x.shape[1]
    n_blocks = idx_eb.shape[0]
    kern = functools.partial(_gather_kernel, cap=cap, seq=seq, batch=batch)
    return pl.pallas_call(
        kern,
        grid=(n_blocks,),
        in_specs=[pl.BlockSpec((1, 1, cap), lambda g: (g, 0, 0), memory_space=pltpu.SMEM),
                  pl.BlockSpec(memory_space=pl.ANY)],
        out_specs=pl.BlockSpec((cap, d), lambda g: (g, 0)),
        out_shape=jax.ShapeDtypeStruct((n_blocks * cap, d), BF16),
        scratch_shapes=[pltpu.VMEM((cap, d), F32), pltpu.SemaphoreType.DMA(())],
        compiler_params=_cparams(("arbitrary",)),
        name="moe_gather",
    )(idx_eb, x)


def _ffn_kernel(x_ref, w1_ref, w3_ref, w2_ref, gate_ref, o_ref, acc):
    f = pl.program_id(2)

    @pl.when(f == 0)
    def _():
        acc[...] = jnp.zeros_like(acc)

    x = x_ref[...]
    h1 = jnp.dot(x, w1_ref[...].astype(BF16), preferred_element_type=F32)
    h3 = jnp.dot(x, w3_ref[...].astype(BF16), preferred_element_type=F32)
    hdn = (h1 * jax.nn.sigmoid(h1)) * h3
    acc[...] += jnp.dot(hdn.astype(BF16), w2_ref[...].astype(BF16), preferred_element_type=F32)

    @pl.when(f == pl.num_programs(2) - 1)
    def _():
        o_ref[...] = acc[...] * gate_ref[...]


def _ffn(xs, w1, w3, w2, gate_col, tm, tf):
    n_e, d, ff = w1.shape
    rows_per_e = xs.shape[0] // n_e
    tm = min(tm, rows_per_e)
    tf = min(tf, ff)
    mt = rows_per_e // tm
    return pl.pallas_call(
        _ffn_kernel,
        grid=(n_e, mt, ff // tf),
        in_specs=[pl.BlockSpec((tm, d), lambda e, m, f: (e * mt + m, 0)),
                  pl.BlockSpec((None, d, tf), lambda e, m, f: (e, 0, f)),
                  pl.BlockSpec((None, d, tf), lambda e, m, f: (e, 0, f)),
                  pl.BlockSpec((None, tf, d), lambda e, m, f: (e, f, 0)),
                  pl.BlockSpec((tm, 1), lambda e, m, f: (e * mt + m, 0))],
        out_specs=pl.BlockSpec((tm, d), lambda e, m, f: (e * mt + m, 0)),
        out_shape=jax.ShapeDtypeStruct((xs.shape[0], d), F32),
        scratch_shapes=[pltpu.VMEM((tm, d), F32)],
        compiler_params=_cparams(("parallel", "parallel", "arbitrary")),
        name="moe_expert_ffn",
    )(xs, w1, w3, w2, gate_col)


def _combine_kernel(idx_ref, y_ref, x_hbm, z_hbm, acc, sem, *, cap, seq, n_e, row_block):
    b = pl.program_id(0)
    e = pl.program_id(1)

    @pl.when(e == 0)
    def _():
        cp = pltpu.make_async_copy(x_hbm.at[pl.ds(b * seq, seq), :], acc, sem)
        cp.start()
        cp.wait()

        def scale(i, carry):
            rows = pl.ds(pl.multiple_of(i * row_block, row_block), row_block)
            acc[rows, :] = acc[rows, :] * DEEPNORM_ALPHA
            return carry

        lax.fori_loop(0, seq // row_block, scale, 0)

    def add_row(s, carry):
        t = idx_ref[0, 0, s]
        acc[pl.ds(t, 1), :] = acc[pl.ds(t, 1), :] + y_ref[pl.ds(s, 1), :]
        return carry

    lax.fori_loop(0, cap, add_row, 0)

    @pl.when(e == n_e - 1)
    def _():
        cp = pltpu.make_async_copy(acc, z_hbm.at[pl.ds(b * seq, seq), :], sem)
        cp.start()
        cp.wait()


def _combine(idx_be, y, x, batch, seq, cap, n_e):
    d = x.shape[1]
    kern = functools.partial(_combine_kernel, cap=cap, seq=seq, n_e=n_e, row_block=min(256, seq))
    return pl.pallas_call(
        kern,
        grid=(batch, n_e),
        in_specs=[pl.BlockSpec((1, 1, cap), lambda b, e: (b * n_e + e, 0, 0), memory_space=pltpu.SMEM),
                  pl.BlockSpec((cap, d), lambda b, e: (e * batch + b, 0)),
                  pl.BlockSpec(memory_space=pl.ANY)],
        out_specs=pl.BlockSpec(memory_space=pl.ANY),
        out_shape=jax.ShapeDtypeStruct((batch * seq, d), F32),
        scratch_shapes=[pltpu.VMEM((seq, d), F32), pltpu.SemaphoreType.DMA(())],
        compiler_params=_cparams(("arbitrary", "arbitrary")),
        name="moe_combine",
    )(idx_be, y, x)


def _rope_tables(seq, rot_dim):
    rows = seq // GRID_W
    row = jnp.repeat(jnp.arange(rows, dtype=jnp.int32), GRID_W).astype(F32)
    col = jnp.tile(jnp.arange(GRID_W, dtype=jnp.int32), rows).astype(F32)
    half = rot_dim // 2
    inv_freq = ROPE_THETA ** (-jnp.arange(0, half, 2, dtype=F32) / half)
    ang = jnp.concatenate([row[:, None] * inv_freq, col[:, None] * inv_freq], axis=-1)
    cos, sin = jnp.cos(ang), jnp.sin(ang)
    cos2 = jnp.repeat(cos, 2, axis=1)
    sin2 = jnp.stack([-sin, sin], axis=-1).reshape(seq, rot_dim)
    pad = LANES - rot_dim
    if pad:
        cos2 = jnp.concatenate([cos2, jnp.ones((seq, pad), F32)], axis=1)
        sin2 = jnp.concatenate([sin2, jnp.zeros((seq, pad), F32)], axis=1)
    return cos2, sin2


def _moe(x1, xb1_unused, logits_t, w1, w3, w2, batch, seq):
    n_e = w1.shape[0]
    cap = CAPACITY_FACTOR * seq // n_e
    idx_col, gate_col = _route(logits_t, batch, seq, cap)
    idx_be = idx_col.reshape(batch * n_e, 1, cap)
    idx_eb = idx_col.reshape(batch, n_e, cap).transpose(1, 0, 2).reshape(n_e * batch, 1, cap)
    gate_eb = gate_col.reshape(batch, n_e, cap).transpose(1, 0, 2).reshape(n_e * batch * cap, 1)
    xs = _gather(idx_eb, x1, batch, seq, cap)
    y = _ffn(xs, w1, w3, w2, gate_eb, tm=1024, tf=256)
    return _combine(idx_be, y, x1, batch, seq, cap, n_e)


def _split_router(w_router):
    wr = w_router.T
    hi = wr.astype(BF16)
    lo = (wr - hi.astype(F32)).astype(BF16)
    return hi, lo


def _even_mixer(xb, batch, seq, w_in, q_norm, w_uq, kv_norm, w_ukv, gate_w2, gate_b, gla_norm,
                cos_r, sin_r):
    d = w_in.shape[0]
    o_cq, o_ckv, o_kr = 0, MLA_Q_RANK, MLA_Q_RANK + MLA_KV_RANK
    o_gq = o_kr + MLA_ROPE_DIM
    o_gk = o_gq + GLA_HEADS * GLA_DK
    o_gv = o_gk + GLA_HEADS * GLA_DK
    o_gr = o_gv + GLA_HEADS * GLA_DV
    o_lat = o_gr + GLA_HEADS * GLA_DV
    o_end = o_lat + 2 * GLA_GATE_RANK
    zeros = lambda n: jnp.zeros((d, n), w_in.dtype)
    tail_pad = LANES - MLA_ROPE_DIM - 2 * GLA_GATE_RANK
    w_in_l = jnp.concatenate([
        w_in[:, o_cq:o_ckv], w_in[:, o_gq:o_gk], w_in[:, o_gk:o_gv], w_in[:, o_ckv:o_kr],
        w_in[:, o_kr:o_gq], w_in[:, o_lat:o_end], zeros(tail_pad), zeros(H_GV - H_TAIL - LANES),
        w_in[:, o_gv:o_gr], w_in[:, o_gr:o_lat]], axis=1).astype(BF16)
    h = _matmul(xb, w_in_l, F32, tm=1024, tn=1024)

    qk = MLA_NOPE_DIM + MLA_ROPE_DIM
    wuq_l = jnp.pad(w_uq.reshape(MLA_Q_RANK, MLA_HEADS, qk),
                    ((0, 0), (0, 0), (0, MLA_QK_PAD - qk))).reshape(MLA_Q_RANK, MLA_HEADS * MLA_QK_PAD)
    wukv3 = w_ukv.reshape(MLA_KV_RANK, MLA_HEADS, MLA_NOPE_DIM + MLA_V_DIM)
    wukv_l = jnp.concatenate([wukv3[:, :, :MLA_NOPE_DIM].reshape(MLA_KV_RANK, -1),
                              wukv3[:, :, MLA_NOPE_DIM:].reshape(MLA_KV_RANK, -1)], axis=1)
    gw = GLA_HEADS * GLA_DK
    wg = jnp.zeros((LANES, 2 * gw), F32)
    wg = wg.at[MLA_ROPE_DIM:MLA_ROPE_DIM + GLA_GATE_RANK, :gw].set(gate_w2[0])
    wg = wg.at[MLA_ROPE_DIM + GLA_GATE_RANK:MLA_ROPE_DIM + 2 * GLA_GATE_RANK, gw:].set(gate_w2[1])
    gb = jnp.concatenate([gate_b[0], gate_b[1]])[None, :]
    q_p, k_p, v_p, la = _mla_prep(h, q_norm[None, :], wuq_l.astype(BF16), kv_norm[None, :],
                                  wukv_l.astype(BF16), wg.astype(BF16), gb, cos_r, sin_r, seq, tm=256)
    o_mla = _attention(q_p, k_p, v_p, batch=batch, seq=seq, kv_heads=MLA_HEADS, group=1,
                       dqk=MLA_QK_PAD, dv=MLA_V_DIM, q_off=0, k_off=0, v_off=0, tq=256)
    o_gla = _gla(h, la, gla_norm.reshape(GLA_HEADS, 1, GLA_DV), batch, seq)
    return o_mla, o_gla


def _odd_mixer(xb, batch, seq, w_qkv, q_norm, k_norm, cos_g, sin_g):
    hd = GQA_HEAD_DIM
    scale = hd ** -0.5
    gains = jnp.concatenate([jnp.tile(q_norm * scale, GQA_HEADS), jnp.tile(k_norm, GQA_KV_HEADS),
                             jnp.ones((GQA_KV_HEADS * hd,), F32)]).reshape(-1, 1, 4 * hd)
    qkv = _qkv_proj(xb, w_qkv.astype(BF16), gains, cos_g, sin_g, seq, tm=1024)
    group = GQA_HEADS // GQA_KV_HEADS
    return _attention(qkv, qkv, qkv, batch=batch, seq=seq, kv_heads=GQA_KV_HEADS, group=group,
                      dqk=hd, dv=hd, q_off=0, k_off=GQA_HEADS, v_off=GQA_HEADS + GQA_KV_HEADS, tq=256)


def kernel(x, mix_w_in, mla_q_norm, mla_w_uq, mla_kv_norm, mla_w_ukv, gla_gate_w2, gla_gate_b,
           gla_out_norm, mix_w_out, gqa_w_qkv, gqa_q_norm, gqa_k_norm, gqa_w_out,
           moe_router, moe_w1, moe_w3, moe_w2, ln_mix_g, ln_mix_b, ln_ffn_g, ln_ffn_b):
    batch, seq, d = x.shape
    cos_r, sin_r = _rope_tables(seq, MLA_ROPE_DIM)
    cos_g, sin_g = _rope_tables(seq, GQA_HEAD_DIM)
    xf = x.reshape(batch * seq, d)
    xb = xf.astype(BF16)
    n_layers = moe_router.shape[0]
    for layer in range(n_layers):
        i = layer // 2
        wr_hi, wr_lo = _split_router(moe_router[layer])
        g_mix, b_mix = ln_mix_g[layer][None, :], ln_mix_b[layer][None, :]
        if layer % 2 == 0:
            o_mla, o_gla = _even_mixer(xb, batch, seq, mix_w_in[i], mla_q_norm[i], mla_w_uq[i],
                                       mla_kv_norm[i], mla_w_ukv[i], gla_gate_w2[i], gla_gate_b[i],
                                       gla_out_norm[i], cos_r, sin_r)
            w_out = mix_w_out[i].astype(BF16)
            n_mla = MLA_HEADS * MLA_V_DIM
            x1, xb1, logits_t = _proj_ln([o_mla, o_gla], [w_out[:n_mla], w_out[n_mla:]], xf,
                                         g_mix, b_mix, wr_hi, wr_lo, tm=256)
        else:
            o = _odd_mixer(xb, batch, seq, gqa_w_qkv[i], gqa_q_norm[i], gqa_k_norm[i], cos_g, sin_g)
            x1, xb1, logits_t = _proj_ln([o], [gqa_w_out[i].astype(BF16)], xf,
                                         g_mix, b_mix, wr_hi, wr_lo, tm=256)
        z = _moe(x1, xb1, logits_t, moe_w1[layer], moe_w3[layer], moe_w2[layer], batch, seq)
        xf, xb = _ln(z, ln_ffn_g[layer][None, :], ln_ffn_b[layer][None, :], tm=512)
    return xf.reshape(batch, seq, d)
```

```python
import functools

import jax
import jax.numpy as jnp
from jax import lax
from jax.experimental import pallas as pl
from jax.experimental.pallas import tpu as pltpu

F32 = jnp.float32
BF16 = jnp.bfloat16

GRID_W = 64
ROPE_THETA = 10000.0
LN_EPS = 1e-5
RMS_EPS = 1e-6
DEPTH = 4
DEEPNORM_ALPHA = (2.0 * DEPTH) ** 0.25

MLA_HEADS = 8
MLA_Q_RANK = 512
MLA_KV_RANK = 256
MLA_NOPE_DIM = 128
MLA_ROPE_DIM = 64
MLA_V_DIM = 128
MLA_QK_PAD = 256

GLA_HEADS = 4
GLA_DK = 128
GLA_DV = 256
GLA_GATE_RANK = 16
GLA_GATE_TAU = 16.0
GLA_CHUNK = 64
GLA_GROUP = 256

GQA_HEADS = 16
GQA_KV_HEADS = 4
GQA_HEAD_DIM = 128

N_EXPERTS = 16
CAPACITY_FACTOR = 2

LANES = 128
VMEM_LIMIT_BYTES = 56 * 1024 * 1024

H_CQ = 0
H_GQ = 512
H_GK = 1024
H_CKV = 1536
H_TAIL = 1792
H_GV = 2048
H_GR = 3072
H_WIDTH = 4096


def _cparams(sem):
    return pltpu.CompilerParams(dimension_semantics=sem, vmem_limit_bytes=VMEM_LIMIT_BYTES)


def _nt_dot(a, b):
    return lax.dot_general(a, b, (((1,), (1,)), ((), ())), preferred_element_type=F32)


def _tn_dot(a, b):
    return lax.dot_general(a, b, (((0,), (0,)), ((), ())), preferred_element_type=F32)


def _rope(x, cos, sin_signed):
    lane = lax.broadcasted_iota(jnp.int32, x.shape, 1)
    partner = jnp.where((lane & 1) == 0, pltpu.roll(x, LANES - 1, 1), pltpu.roll(x, 1, 1))
    return x * cos + partner * sin_signed


def _rms(x, gain):
    return x * lax.rsqrt(jnp.mean(x * x, axis=-1, keepdims=True) + RMS_EPS) * gain


def _layer_norm(z, g, b):
    mu = jnp.mean(z, axis=-1, keepdims=True)
    zc = z - mu
    var = jnp.mean(zc * zc, axis=-1, keepdims=True)
    return zc * lax.rsqrt(var + LN_EPS) * g + b


def _mm_kernel(x_ref, w_ref, o_ref):
    o_ref[...] = jnp.dot(x_ref[...], w_ref[...], preferred_element_type=F32).astype(o_ref.dtype)


def _matmul(x, w, out_dtype, tm, tn):
    m, k = x.shape
    n = w.shape[1]
    tm, tn = min(tm, m), min(tn, n)
    return pl.pallas_call(
        _mm_kernel,
        grid=(n // tn, m // tm),
        in_specs=[pl.BlockSpec((tm, k), lambda j, i: (i, 0)),
                  pl.BlockSpec((k, tn), lambda j, i: (0, j))],
        out_specs=pl.BlockSpec((tm, tn), lambda j, i: (i, j)),
        out_shape=jax.ShapeDtypeStruct((m, n), out_dtype),
        compiler_params=_cparams(("parallel", "parallel")),
        name="dense_matmul",
    )(x, w)


def _qkv_kernel(x_ref, w_ref, g_ref, cos_ref, sin_ref, o_ref, *, n_normed_tiles, heads_per_tile):
    acc = jnp.dot(x_ref[...], w_ref[...], preferred_element_type=F32)
    j = pl.program_id(0)

    @pl.when(j < n_normed_tiles)
    def _():
        cos = cos_ref[...]
        sin = sin_ref[...]
        for hd in range(heads_per_tile):
            sl = slice(hd * GQA_HEAD_DIM, (hd + 1) * GQA_HEAD_DIM)
            y = _rms(acc[:, sl], g_ref[:, sl])
            o_ref[:, sl] = _rope(y, cos, sin).astype(o_ref.dtype)

    @pl.when(j >= n_normed_tiles)
    def _():
        o_ref[...] = acc.astype(o_ref.dtype)


def _qkv_proj(xb, w, gains, cos, sin, seq, tm):
    m, k = xb.shape
    n = w.shape[1]
    tn = 4 * GQA_HEAD_DIM
    tm = min(tm, seq)
    n_tiles = n // tn
    pos_blocks = seq // tm
    kern = functools.partial(_qkv_kernel, n_normed_tiles=n_tiles - 1, heads_per_tile=4)
    return pl.pallas_call(
        kern,
        grid=(n_tiles, m // tm),
        in_specs=[pl.BlockSpec((tm, k), lambda j, i: (i, 0)),
                  pl.BlockSpec((k, tn), lambda j, i: (0, j)),
                  pl.BlockSpec((None, 1, tn), lambda j, i: (j, 0, 0)),
                  pl.BlockSpec((tm, LANES), lambda j, i: (i % pos_blocks, 0)),
                  pl.BlockSpec((tm, LANES), lambda j, i: (i % pos_blocks, 0))],
        out_specs=pl.BlockSpec((tm, tn), lambda j, i: (i, j)),
        out_shape=jax.ShapeDtypeStruct((m, n), BF16),
        compiler_params=_cparams(("parallel", "parallel")),
        name="gqa_qkv_proj",
    )(xb, w, gains, cos, sin)


def _attn_kernel(q_ref, k_ref, v_ref, o_ref, *, group, dqk, dv, kc):
    tq = q_ref.shape[0]
    seq = k_ref.shape[0]
    q = jnp.concatenate([q_ref[:, g * dqk:(g + 1) * dqk] for g in range(group)], axis=0)
    rows = group * tq
    m = jnp.full((rows, 1), -jnp.inf, F32)
    l = jnp.zeros((rows, 1), F32)
    acc = jnp.zeros((rows, dv), F32)
    for c in range(seq // kc):
        s = _nt_dot(q, k_ref[c * kc:(c + 1) * kc, :])
        m_new = jnp.maximum(m, jnp.max(s, axis=1, keepdims=True))
        p = jnp.exp(s - m_new)
        corr = jnp.exp(m - m_new)
        l = l * corr + jnp.sum(p, axis=1, keepdims=True)
        acc = acc * corr + jnp.dot(p.astype(BF16), v_ref[c * kc:(c + 1) * kc, :],
                                   preferred_element_type=F32)
        m = m_new
    o = acc / l
    for g in range(group):
        o_ref[:, g * dv:(g + 1) * dv] = o[g * tq:(g + 1) * tq, :].astype(o_ref.dtype)


def _attention(q_arr, k_arr, v_arr, *, batch, seq, kv_heads, group, dqk, dv,
               q_off, k_off, v_off, tq):
    tq = min(tq, seq)
    nq = seq // tq
    kern = functools.partial(_attn_kernel, group=group, dqk=dqk, dv=dv, kc=min(1024, seq))
    return pl.pallas_call(
        kern,
        grid=(batch, kv_heads, nq),
        in_specs=[pl.BlockSpec((tq, group * dqk), lambda b, h, i: (b * nq + i, q_off + h)),
                  pl.BlockSpec((seq, dqk), lambda b, h, i: (b, k_off + h)),
                  pl.BlockSpec((seq, dv), lambda b, h, i: (b, v_off + h))],
        out_specs=pl.BlockSpec((tq, group * dv), lambda b, h, i: (b * nq + i, h)),
        out_shape=jax.ShapeDtypeStruct((batch * seq, kv_heads * group * dv), BF16),
        compiler_params=_cparams(("parallel", "parallel", "parallel")),
        name="softmax_attention",
    )(q_arr, k_arr, v_arr)


def _mla_prep_kernel(cq_ref, ckv_ref, tail_ref, qn_ref, wuq_ref, kvn_ref, wukv_ref, wg_ref, gb_ref,
                     cos_ref, sin_ref, q_ref, k_ref, v_ref, la_ref):
    cos = cos_ref[...]
    sin = sin_ref[...]
    scale = (MLA_NOPE_DIM + MLA_ROPE_DIM) ** -0.5
    q = jnp.dot(_rms(cq_ref[...], qn_ref[...]).astype(BF16), wuq_ref[...],
                preferred_element_type=F32) * scale
    kv = jnp.dot(_rms(ckv_ref[...], kvn_ref[...]).astype(BF16), wukv_ref[...],
                 preferred_element_type=F32)
    tail = tail_ref[...]
    lane = lax.broadcasted_iota(jnp.int32, tail.shape, 1)
    k_pe = jnp.where(lane < MLA_ROPE_DIM, _rope(tail, cos, sin), 0.0).astype(BF16)
    for hd in range(MLA_HEADS):
        c0 = hd * MLA_QK_PAD
        q_ref[:, c0:c0 + LANES] = q[:, c0:c0 + LANES].astype(BF16)
        q_ref[:, c0 + LANES:c0 + 2 * LANES] = _rope(q[:, c0 + LANES:c0 + 2 * LANES], cos, sin).astype(BF16)
        k_ref[:, c0:c0 + LANES] = kv[:, hd * LANES:(hd + 1) * LANES].astype(BF16)
        k_ref[:, c0 + LANES:c0 + 2 * LANES] = k_pe
    nv = MLA_HEADS * MLA_V_DIM
    v_ref[...] = kv[:, nv:].astype(BF16)
    gate = jnp.dot(tail.astype(BF16), wg_ref[...], preferred_element_type=F32) + gb_ref[...]
    log_sig = jnp.minimum(gate, 0.0) - jnp.log1p(jnp.exp(-jnp.abs(gate)))
    la_ref[...] = log_sig / GLA_GATE_TAU


def _mla_prep(h, qn, wuq, kvn, wukv, wg, gb, cos, sin, seq, tm):
    m = h.shape[0]
    tm = min(tm, seq)
    pos_blocks = seq // tm
    qw = MLA_HEADS * MLA_QK_PAD
    vw = MLA_HEADS * MLA_V_DIM
    gw = 2 * GLA_HEADS * GLA_DK
    const = lambda i: (0, 0)
    return pl.pallas_call(
        _mla_prep_kernel,
        grid=(m // tm,),
        in_specs=[pl.BlockSpec((tm, MLA_Q_RANK), lambda i: (i, H_CQ // MLA_Q_RANK)),
                  pl.BlockSpec((tm, MLA_KV_RANK), lambda i: (i, H_CKV // MLA_KV_RANK)),
                  pl.BlockSpec((tm, LANES), lambda i: (i, H_TAIL // LANES)),
                  pl.BlockSpec((1, MLA_Q_RANK), const),
                  pl.BlockSpec((MLA_Q_RANK, qw), const),
                  pl.BlockSpec((1, MLA_KV_RANK), const),
                  pl.BlockSpec((MLA_KV_RANK, 2 * vw), const),
                  pl.BlockSpec((LANES, gw), const),
                  pl.BlockSpec((1, gw), const),
                  pl.BlockSpec((tm, LANES), lambda i: (i % pos_blocks, 0)),
                  pl.BlockSpec((tm, LANES), lambda i: (i % pos_blocks, 0))],
        out_specs=[pl.BlockSpec((tm, qw), lambda i: (i, 0)),
                   pl.BlockSpec((tm, qw), lambda i: (i, 0)),
                   pl.BlockSpec((tm, vw), lambda i: (i, 0)),
                   pl.BlockSpec((tm, gw), lambda i: (i, 0))],
        out_shape=[jax.ShapeDtypeStruct((m, qw), BF16),
                   jax.ShapeDtypeStruct((m, qw), BF16),
                   jax.ShapeDtypeStruct((m, vw), BF16),
                   jax.ShapeDtypeStruct((m, gw), F32)],
        compiler_params=_cparams(("parallel",)),
        name="mla_prep",
    )(h, h, h, qn, wuq, kvn, wukv, wg, gb, cos, sin)


def _split3(x):
    hi = x.astype(BF16)
    r1 = x - hi.astype(F32)
    mid = r1.astype(BF16)
    lo = (r1 - mid.astype(F32)).astype(BF16)
    return hi, mid, lo


def _gla_kernel(q_ref, k_ref, v_ref, gr_ref, laf_ref, lab_ref, gn_ref, o_ref,
                of_s, ob_s, qf_s, qb_s, kf_s, kb_s, df_s, db_s, stf_s, stb_s):
    seq = q_ref.shape[0]
    c = GLA_CHUNK
    r = min(GLA_GROUP, seq)
    cpg = r // c
    n_groups = seq // r
    n_chunks = seq // c
    scale = GLA_DK ** -0.5

    row = lax.broadcasted_iota(jnp.int32, (r, r), 0)
    col = lax.broadcasted_iota(jnp.int32, (r, r), 1)
    same = (row // c) == (col // c)
    tril = same & (col <= row)
    triu = same & (col >= row)
    tril_b = tril.astype(F32).astype(BF16)
    triu_b = triu.astype(F32).astype(BF16)

    def group_body(gi, carry):
        r0 = pl.multiple_of(gi * r, r)
        rows = pl.ds(r0, r)
        q = q_ref[rows, :] * scale
        k = k_ref[rows, :]
        v = v_ref[rows, :].astype(BF16)
        for la_ref, mask, mask_b, edge, o_s, q_s, k_s, d_s in (
                (laf_ref, tril, tril_b, c - 1, of_s, qf_s, kf_s, df_s),
                (lab_ref, triu, triu_b, 0, ob_s, qb_s, kb_s, db_s)):
            hi, mid, lo = _split3(la_ref[rows, :])
            b = (jnp.dot(mask_b, hi, preferred_element_type=F32)
                 + jnp.dot(mask_b, mid, preferred_element_type=F32)
                 + jnp.dot(mask_b, lo, preferred_element_type=F32))
            b_edge = jnp.concatenate(
                [jnp.broadcast_to(b[ci * c + edge:ci * c + edge + 1, :], (c, GLA_DK)) for ci in range(cpg)],
                axis=0)
            q_in = (q * jnp.exp(b)).astype(BF16)
            k_in = (k * jnp.exp(-b)).astype(BF16)
            k_st = (k * jnp.exp(b_edge - b)).astype(BF16)
            att = jnp.where(mask, _nt_dot(q_in, k_in), 0.0)
            o_s[rows, :] = jnp.dot(att.astype(BF16), v, preferred_element_type=F32)
            q_s[rows, :] = q_in
            k_s[rows, :] = k_st
            for ci in range(cpg):
                d_s[pl.ds(gi * cpg + ci, 1), :] = jnp.exp(b[ci * c + edge:ci * c + edge + 1, :])
        return carry

    lax.fori_loop(0, n_groups, group_body, 0)

    stf_s[...] = jnp.zeros_like(stf_s)
    stb_s[...] = jnp.zeros_like(stb_s)

    def chunk_body(i, carry):
        for n, o_s, q_s, k_s, d_s, st_s in ((i, of_s, qf_s, kf_s, df_s, stf_s),
                                            (n_chunks - 1 - i, ob_s, qb_s, kb_s, db_s, stb_s)):
            rows = pl.ds(pl.multiple_of(n * c, c), c)
            st = st_s[...]
            o_s[rows, :] += _nt_dot(q_s[rows, :], st.astype(BF16))
            ds = _tn_dot(v_ref[rows, :].astype(BF16), k_s[rows, :])
            st_s[...] = d_s[pl.ds(n, 1), :] * st + ds
        return carry

    lax.fori_loop(0, n_chunks, chunk_body, 0)

    gain = gn_ref[...]

    def out_body(gi, carry):
        rows = pl.ds(pl.multiple_of(gi * r, r), r)
        o = _rms(of_s[rows, :] + ob_s[rows, :], gain)
        gr = gr_ref[rows, :]
        o_ref[rows, :] = (o * (gr * jax.nn.sigmoid(gr))).astype(o_ref.dtype)
        return carry

    lax.fori_loop(0, n_groups, out_body, 0)


def _gla(h, la, gn, batch, seq):
    dk, dv, nh = GLA_DK, GLA_DV, GLA_HEADS
    n_chunks = seq // GLA_CHUNK
    return pl.pallas_call(
        _gla_kernel,
        grid=(batch, nh),
        in_specs=[pl.BlockSpec((seq, dk), lambda b, hd: (b, H_GQ // dk + hd)),
                  pl.BlockSpec((seq, dk), lambda b, hd: (b, H_GK // dk + hd)),
                  pl.BlockSpec((seq, dv), lambda b, hd: (b, H_GV // dv + hd)),
                  pl.BlockSpec((seq, dv), lambda b, hd: (b, H_GR // dv + hd)),
                  pl.BlockSpec((seq, dk), lambda b, hd: (b, hd)),
                  pl.BlockSpec((seq, dk), lambda b, hd: (b, nh + hd)),
                  pl.BlockSpec((None, 1, dv), lambda b, hd: (hd, 0, 0))],
        out_specs=pl.BlockSpec((seq, dv), lambda b, hd: (b, hd)),
        out_shape=jax.ShapeDtypeStruct((batch * seq, nh * dv), BF16),
        scratch_shapes=[pltpu.VMEM((seq, dv), F32), pltpu.VMEM((seq, dv), F32),
                        pltpu.VMEM((seq, dk), BF16), pltpu.VMEM((seq, dk), BF16),
                        pltpu.VMEM((seq, dk), BF16), pltpu.VMEM((seq, dk), BF16),
                        pltpu.VMEM((n_chunks, dk), F32), pltpu.VMEM((n_chunks, dk), F32),
                        pltpu.VMEM((dv, dk), F32), pltpu.VMEM((dv, dk), F32)],
        compiler_params=_cparams(("arbitrary", "arbitrary")),
        name="gla_bidirectional",
    )(h, h, h, h, la, la, gn)


def _proj_ln_kernel(*refs, n_in):
    a_refs = refs[:n_in]
    w_refs = refs[n_in:2 * n_in]
    x_ref, g_ref, b_ref, wrh_ref, wrl_ref, xo_ref, xp_ref, lg_ref = refs[2 * n_in:]
    mix = jnp.dot(a_refs[0][...], w_refs[0][...], preferred_element_type=F32)
    for a_ref, w_ref in zip(a_refs[1:], w_refs[1:]):
        mix = mix + jnp.dot(a_ref[...], w_ref[...], preferred_element_type=F32)
    y = _layer_norm(DEEPNORM_ALPHA * x_ref[...] + mix, g_ref[...], b_ref[...])
    xo_ref[...] = y
    y_hi = y.astype(BF16)
    y_hi32 = y_hi.astype(F32)
    half = y.shape[1] // 2
    bits = lax.bitcast_convert_type(y_hi32, jnp.int32)
    xp_ref[...] = lax.shift_right_logical(bits[:, :half], 16) | bits[:, half:]
    y_lo = (y - y_hi32).astype(BF16)
    wrh = wrh_ref[...]
    lg_ref[...] = _nt_dot(wrh, y_hi) + _nt_dot(wrh, y_lo) + _nt_dot(wrl_ref[...], y_hi)


def _proj_ln(acts, weights, x, g, b, wr_hi, wr_lo, tm):
    m, d = x.shape
    tm = min(tm, m)
    n_in = len(acts)
    n_e = wr_hi.shape[0]
    const = lambda i: (0, 0)
    in_specs = ([pl.BlockSpec((tm, a.shape[1]), lambda i: (i, 0)) for a in acts]
                + [pl.BlockSpec(w.shape, const) for w in weights]
                + [pl.BlockSpec((tm, d), lambda i: (i, 0)),
                   pl.BlockSpec((1, d), const), pl.BlockSpec((1, d), const),
                   pl.BlockSpec((n_e, d), const), pl.BlockSpec((n_e, d), const)])
    return pl.pallas_call(
        functools.partial(_proj_ln_kernel, n_in=n_in),
        grid=(m // tm,),
        in_specs=in_specs,
        out_specs=[pl.BlockSpec((tm, d), lambda i: (i, 0)),
                   pl.BlockSpec((tm, d // 2), lambda i: (i, 0)),
                   pl.BlockSpec((n_e, tm), lambda i: (0, i))],
        out_shape=[jax.ShapeDtypeStruct((m, d), F32),
                   jax.ShapeDtypeStruct((m, d // 2), jnp.int32),
                   jax.ShapeDtypeStruct((n_e, m), F32)],
        compiler_params=_cparams(("parallel",)),
        name="out_proj_layernorm_router",
    )(*acts, *weights, x, g, b, wr_hi, wr_lo)


def _ln_kernel(z_ref, g_ref, b_ref, xo_ref, xb_ref):
    y = _layer_norm(z_ref[...], g_ref[...], b_ref[...])
    xo_ref[...] = y
    xb_ref[...] = y.astype(BF16)


def _ln(z, g, b, tm):
    m, d = z.shape
    tm = min(tm, m)
    const = lambda i: (0, 0)
    return pl.pallas_call(
        _ln_kernel,
        grid=(m // tm,),
        in_specs=[pl.BlockSpec((tm, d), lambda i: (i, 0)),
                  pl.BlockSpec((1, d), const), pl.BlockSpec((1, d), const)],
        out_specs=[pl.BlockSpec((tm, d), lambda i: (i, 0)),
                   pl.BlockSpec((tm, d), lambda i: (i, 0))],
        out_shape=[jax.ShapeDtypeStruct((m, d), F32), jax.ShapeDtypeStruct((m, d), BF16)],
        compiler_params=_cparams(("parallel",)),
        name="layernorm",
    )(z, g, b)


def _route_kernel(lg_ref, idx_ref, gate_ref, aff_s, cum_s, start_s, *, cap):
    lg = lg_ref[...]
    ex = jnp.exp(lg - jnp.max(lg, axis=0, keepdims=True))
    aff = ex / jnp.sum(ex, axis=0, keepdims=True)
    bits = lax.bitcast_convert_type(aff, jnp.int32)
    n_e, seq = lg.shape
    nblk = seq // LANES
    capf = float(cap)

    def bisect(_, carry):
        lo, hi = carry
        mid = lo + ((hi - lo + 1) >> 1)
        cnt = jnp.sum(jnp.where(bits >= mid, 1.0, 0.0), axis=1, keepdims=True)
        ok = cnt >= capf
        return jnp.where(ok, mid, lo), jnp.where(ok, hi, mid - 1)

    lo0 = jnp.zeros((n_e, 1), jnp.int32)
    hi0 = jnp.full((n_e, 1), 0x7F800000, jnp.int32)
    thr, _ = lax.fori_loop(0, 32, bisect, (lo0, hi0))

    gt = bits > thr
    eq = bits == thr
    need = capf - jnp.sum(jnp.where(gt, 1.0, 0.0), axis=1, keepdims=True)

    r_i = lax.broadcasted_iota(jnp.int32, (LANES, LANES), 0)
    c_i = lax.broadcasted_iota(jnp.int32, (LANES, LANES), 1)
    upper = (r_i <= c_i).astype(F32).astype(BF16)
    eq_off = jnp.zeros((n_e, 1), F32)
    sel_off = jnp.zeros((n_e, 1), F32)
    for j in range(nblk):
        blk = slice(j * LANES, (j + 1) * LANES)
        eq01 = jnp.where(eq[:, blk], 1.0, 0.0)
        eq_cum = jnp.dot(eq01.astype(BF16), upper, preferred_element_type=F32) + eq_off
        eq_off = eq_cum[:, LANES - 1:LANES]
        tie_rank = eq_cum - eq01
        sel01 = jnp.where(gt[:, blk] | (eq[:, blk] & (tie_rank < need)), 1.0, 0.0)
        sel_cum = jnp.dot(sel01.astype(BF16), upper, preferred_element_type=F32) + sel_off
        rows = slice(j * n_e, (j + 1) * n_e)
        cum_s[rows, :] = sel_cum
        start_s[rows, :] = jnp.broadcast_to(sel_off, (n_e, LANES))
        aff_s[rows, :] = aff[:, blk]
        sel_off = sel_cum[:, LANES - 1:LANES]

    slot_row = lax.broadcasted_iota(jnp.int32, (1, cap), 1).astype(F32)
    slot_col = lax.broadcasted_iota(jnp.int32, (cap, 1), 0).astype(F32)
    lane = lax.broadcasted_iota(jnp.int32, (cap, LANES), 1).astype(F32)
    blk_id = lax.broadcasted_iota(jnp.int32, (nblk, LANES), 0).astype(F32).astype(BF16)

    def extract(e, carry):
        rows = pl.ds(e, nblk, stride=n_e)
        cum_e = cum_s[rows, :]
        start_e = start_s[rows, :][:, 0:1]
        end_e = cum_e[:, LANES - 1:LANES]
        in_blk = jnp.where((start_e <= slot_row) & (slot_row < end_e), 1.0, 0.0).astype(BF16)
        cum_hi = jnp.floor(cum_e * (1.0 / 32.0))
        cum_lo = cum_e - 32.0 * cum_hi
        g = 32.0 * _tn_dot(in_blk, cum_hi.astype(BF16)) + _tn_dot(in_blk, cum_lo.astype(BF16))
        blk_of_slot = _tn_dot(in_blk, blk_id)[:, 0:1]
        pos = jnp.sum(jnp.where(g <= slot_col, 1.0, 0.0), axis=1, keepdims=True)
        a_hi, a_mid, a_lo = _split3(aff_s[rows, :])
        a = (_tn_dot(in_blk, a_hi) + _tn_dot(in_blk, a_mid)) + _tn_dot(in_blk, a_lo)
        gate = jnp.sum(jnp.where(lane == pos, a, 0.0), axis=1, keepdims=True)
        out_rows = pl.ds(pl.multiple_of(e * cap, cap), cap)
        idx_ref[out_rows, :] = (float(LANES) * blk_of_slot + pos).astype(jnp.int32)
        gate_ref[out_rows, :] = gate
        return carry

    lax.fori_loop(0, n_e, extract, 0)


def _route(logits_t, batch, seq, cap):
    n_e = logits_t.shape[0]
    kern = functools.partial(_route_kernel, cap=cap)
    scratch_rows = (seq // LANES) * n_e
    return pl.pallas_call(
        kern,
        grid=(batch,),
        in_specs=[pl.BlockSpec((n_e, seq), lambda b: (0, b))],
        out_specs=[pl.BlockSpec((n_e * cap, 1), lambda b: (b, 0)),
                   pl.BlockSpec((n_e * cap, 1), lambda b: (b, 0))],
        out_shape=[jax.ShapeDtypeStruct((batch * n_e * cap, 1), jnp.int32),
                   jax.ShapeDtypeStruct((batch * n_e * cap, 1), F32)],
        scratch_shapes=[pltpu.VMEM((scratch_rows, LANES), F32), pltpu.VMEM((scratch_rows, LANES), F32),
                        pltpu.VMEM((scratch_rows, LANES), F32)],
        compiler_params=_cparams(("parallel",)),
        name="expert_choice_route",
    )(logits_t)


GATHER_UNROLL = 8


def _ffn_kernel(idx_cur, idx_nxt, xp_hbm, w1_ref, w3_ref, w2_ref, gate_ref, o_ref,
                land, x_lo, x_hi, acc, sem, *, cap, seq, bpt, issue_steps, unpack_rows):
    m_tiles = pl.num_programs(1)
    f = pl.program_id(2)
    t = pl.program_id(0) * m_tiles + pl.program_id(1)
    n_tiles = pl.num_programs(0) * m_tiles
    slot = t % 2
    tm = bpt * cap
    half = x_lo.shape[1]

    def row_copy(dst_slot, r, src_row):
        return pltpu.make_async_copy(xp_hbm.at[pl.ds(src_row, 1), :],
                                     land.at[dst_slot, pl.ds(r, 1), :], sem.at[dst_slot])

    def issue(idx_ref, tile, dst_slot, bb, s0, n):
        src_base = ((tile % m_tiles) * bpt + bb) * seq
        dst_base = bb * cap

        def body(i, carry):
            for u in range(GATHER_UNROLL):
                s = s0 + i * GATHER_UNROLL + u
                row_copy(dst_slot, dst_base + s, src_base + idx_ref[bb, 0, s]).start()
            return carry

        lax.fori_loop(0, n // GATHER_UNROLL, body, 0)

    @pl.when((t == 0) & (f == 0))
    def _():
        for bb in range(bpt):
            issue(idx_cur, t, slot, bb, 0, cap)

    @pl.when(f == 0)
    def _():
        pltpu.make_async_copy(xp_hbm.at[pl.ds(0, tm), :], land.at[slot], sem.at[slot]).wait()

        def unpack(i, carry):
            rows = pl.ds(pl.multiple_of(i * unpack_rows, unpack_rows), unpack_rows)
            w = land[slot, rows, :]
            x_lo[rows, :] = lax.bitcast_convert_type(w << 16, F32).astype(BF16)
            x_hi[rows, :] = lax.bitcast_convert_type(w & jnp.int32(-65536), F32).astype(BF16)
            return carry

        lax.fori_loop(0, tm // unpack_rows, unpack, 0)
        acc[...] = jnp.zeros_like(acc)

    per_batch = issue_steps // bpt
    n = cap // per_batch
    for bb in range(bpt):
        @pl.when((f >= bb * per_batch) & (f < (bb + 1) * per_batch) & (t + 1 < n_tiles))
        def _(bb=bb):
            issue(idx_nxt, t + 1, 1 - slot, bb, (f - bb * per_batch) * n, n)

    w1 = w1_ref[...].astype(BF16)
    w3 = w3_ref[...].astype(BF16)
    lo, hi = x_lo[...], x_hi[...]
    h1 = (jnp.dot(lo, w1[:half], preferred_element_type=F32)
          + jnp.dot(hi, w1[half:], preferred_element_type=F32))
    h3 = (jnp.dot(lo, w3[:half], preferred_element_type=F32)
          + jnp.dot(hi, w3[half:], preferred_element_type=F32))
    hdn = (h1 * jax.nn.sigmoid(h1)) * h3
    acc[...] += jnp.dot(hdn.astype(BF16), w2_ref[...].astype(BF16), preferred_element_type=F32)

    @pl.when(f == pl.num_programs(2) - 1)
    def _():
        o_ref[...] = acc[...] * gate_ref[...]


def _ffn(idx_eb, xp, w1, w3, w2, layer, gate_col, batch, seq, cap, tf):
    _, n_e, d, ff = w1.shape
    bpt = min(2, batch)
    tm = bpt * cap
    mt = batch // bpt
    tf = min(tf, ff // bpt)
    nf = ff // tf
    issue_steps = bpt * min(2, nf // bpt)
    assert cap % (issue_steps // bpt * GATHER_UNROLL) == 0
    n_tiles = n_e * mt
    kern = functools.partial(_ffn_kernel, cap=cap, seq=seq, bpt=bpt, issue_steps=issue_steps,
                             unpack_rows=min(256, tm))
    smem_idx = lambda fn: pl.BlockSpec((bpt, 1, cap), fn, memory_space=pltpu.SMEM)
    return pl.pallas_call(
        kern,
        grid=(n_e, mt, nf),
        in_specs=[smem_idx(lambda e, m, f: (e * mt + m, 0, 0)),
                  smem_idx(lambda e, m, f: (jnp.minimum(e * mt + m + 1, n_tiles - 1), 0, 0)),
                  pl.BlockSpec(memory_space=pl.ANY),
                  pl.BlockSpec((None, None, d, tf), lambda e, m, f: (layer, e, 0, f)),
                  pl.BlockSpec((None, None, d, tf), lambda e, m, f: (layer, e, 0, f)),
                  pl.BlockSpec((None, None, tf, d), lambda e, m, f: (layer, e, f, 0)),
                  pl.BlockSpec((tm, 1), lambda e, m, f: (e * mt + m, 0))],
        out_specs=pl.BlockSpec((tm, d), lambda e, m, f: (e * mt + m, 0)),
        out_shape=jax.ShapeDtypeStruct((n_e * batch * cap, d), F32),
        scratch_shapes=[pltpu.VMEM((2, tm, d // 2), jnp.int32),
                        pltpu.VMEM((tm, d // 2), BF16), pltpu.VMEM((tm, d // 2), BF16),
                        pltpu.VMEM((tm, d), F32), pltpu.SemaphoreType.DMA((2,))],
        compiler_params=_cparams(("arbitrary", "arbitrary", "arbitrary")),
        name="moe_expert_ffn",
    )(idx_eb, idx_eb, xp, w1, w3, w2, gate_col)


COMBINE_UNROLL = 8


def _combine_kernel(idx_ref, y_ref, x_hbm, z_hbm, acc, sem, *, cap, seq, n_e, row_block):
    b = pl.program_id(0)
    e = pl.program_id(1)

    @pl.when(e == 0)
    def _():
        cp = pltpu.make_async_copy(x_hbm.at[pl.ds(b * seq, seq), :], acc, sem)
        cp.start()
        cp.wait()

        def scale(i, carry):
            rows = pl.ds(pl.multiple_of(i * row_block, row_block), row_block)
            acc[rows, :] = acc[rows, :] * DEEPNORM_ALPHA
            return carry

        lax.fori_loop(0, seq // row_block, scale, 0)

    def add_rows(i, carry):
        s0 = i * COMBINE_UNROLL
        toks = [idx_ref[0, 0, s0 + u] for u in range(COMBINE_UNROLL)]
        sums = [acc[pl.ds(toks[u], 1), :] + y_ref[pl.ds(s0 + u, 1), :] for u in range(COMBINE_UNROLL)]
        for u in range(COMBINE_UNROLL):
            acc[pl.ds(toks[u], 1), :] = sums[u]
        return carry

    lax.fori_loop(0, cap // COMBINE_UNROLL, add_rows, 0)

    @pl.when(e == n_e - 1)
    def _():
        cp = pltpu.make_async_copy(acc, z_hbm.at[pl.ds(b * seq, seq), :], sem)
        cp.start()
        cp.wait()


def _combine(idx_be, y, x, batch, seq, cap, n_e):
    d = x.shape[1]
    kern = functools.partial(_combine_kernel, cap=cap, seq=seq, n_e=n_e, row_block=min(256, seq))
    return pl.pallas_call(
        kern,
        grid=(batch, n_e),
        in_specs=[pl.BlockSpec((1, 1, cap), lambda b, e: (b * n_e + e, 0, 0), memory_space=pltpu.SMEM),
                  pl.BlockSpec((cap, d), lambda b, e: (e * batch + b, 0)),
                  pl.BlockSpec(memory_space=pl.ANY)],
        out_specs=pl.BlockSpec(memory_space=pl.ANY),
        out_shape=jax.ShapeDtypeStruct((batch * seq, d), F32),
        scratch_shapes=[pltpu.VMEM((seq, d), F32), pltpu.SemaphoreType.DMA(())],
        compiler_params=_cparams(("arbitrary", "arbitrary")),
        name="moe_combine",
    )(idx_be, y, x)


def _rope_tables(seq, rot_dim):
    rows = seq // GRID_W
    row = jnp.repeat(jnp.arange(rows, dtype=jnp.int32), GRID_W).astype(F32)
    col = jnp.tile(jnp.arange(GRID_W, dtype=jnp.int32), rows).astype(F32)
    half = rot_dim // 2
    inv_freq = ROPE_THETA ** (-jnp.arange(0, half, 2, dtype=F32) / half)
    ang = jnp.concatenate([row[:, None] * inv_freq, col[:, None] * inv_freq], axis=-1)
    cos, sin = jnp.cos(ang), jnp.sin(ang)
    cos2 = jnp.repeat(cos, 2, axis=1)
    sin2 = jnp.stack([-sin, sin], axis=-1).reshape(seq, rot_dim)
    pad = LANES - rot_dim
    if pad:
        cos2 = jnp.concatenate([cos2, jnp.ones((seq, pad), F32)], axis=1)
        sin2 = jnp.concatenate([sin2, jnp.zeros((seq, pad), F32)], axis=1)
    return cos2, sin2


def _moe(x1, xp1, logits_t, w1, w3, w2, layer, batch, seq):
    n_e = w1.shape[1]
    cap = CAPACITY_FACTOR * seq // n_e
    idx_col, gate_col = _route(logits_t, batch, seq, cap)
    idx_be = idx_col.reshape(batch * n_e, 1, cap)
    idx_eb = idx_col.reshape(batch, n_e, cap).transpose(1, 0, 2).reshape(n_e * batch, 1, cap)
    gate_eb = gate_col.reshape(batch, n_e, cap).transpose(1, 0, 2).reshape(n_e * batch * cap, 1)
    y = _ffn(idx_eb, xp1, w1, w3, w2, layer, gate_eb, batch, seq, cap, tf=256)
    return _combine(idx_be, y, x1, batch, seq, cap, n_e)


def _split_router(w_router):
    wr = w_router.T
    hi = wr.astype(BF16)
    lo = (wr - hi.astype(F32)).astype(BF16)
    return hi, lo


def _even_mixer(xb, batch, seq, w_in, q_norm, w_uq, kv_norm, w_ukv, gate_w2, gate_b, gla_norm,
                cos_r, sin_r):
    d = w_in.shape[0]
    o_cq, o_ckv, o_kr = 0, MLA_Q_RANK, MLA_Q_RANK + MLA_KV_RANK
    o_gq = o_kr + MLA_ROPE_DIM
    o_gk = o_gq + GLA_HEADS * GLA_DK
    o_gv = o_gk + GLA_HEADS * GLA_DK
    o_gr = o_gv + GLA_HEADS * GLA_DV
    o_lat = o_gr + GLA_HEADS * GLA_DV
    o_end = o_lat + 2 * GLA_GATE_RANK
    zeros = lambda n: jnp.zeros((d, n), w_in.dtype)
    tail_pad = LANES - MLA_ROPE_DIM - 2 * GLA_GATE_RANK
    w_in_l = jnp.concatenate([
        w_in[:, o_cq:o_ckv], w_in[:, o_gq:o_gk], w_in[:, o_gk:o_gv], w_in[:, o_ckv:o_kr],
        w_in[:, o_kr:o_gq], w_in[:, o_lat:o_end], zeros(tail_pad), zeros(H_GV - H_TAIL - LANES),
        w_in[:, o_gv:o_gr], w_in[:, o_gr:o_lat]], axis=1).astype(BF16)
    h = _matmul(xb, w_in_l, F32, tm=1024, tn=1024)

    qk = MLA_NOPE_DIM + MLA_ROPE_DIM
    wuq_l = jnp.pad(w_uq.reshape(MLA_Q_RANK, MLA_HEADS, qk),
                    ((0, 0), (0, 0), (0, MLA_QK_PAD - qk))).reshape(MLA_Q_RANK, MLA_HEADS * MLA_QK_PAD)
    wukv3 = w_ukv.reshape(MLA_KV_RANK, MLA_HEADS, MLA_NOPE_DIM + MLA_V_DIM)
    wukv_l = jnp.concatenate([wukv3[:, :, :MLA_NOPE_DIM].reshape(MLA_KV_RANK, -1),
                              wukv3[:, :, MLA_NOPE_DIM:].reshape(MLA_KV_RANK, -1)], axis=1)
    gw = GLA_HEADS * GLA_DK
    wg = jnp.zeros((LANES, 2 * gw), F32)
    wg = wg.at[MLA_ROPE_DIM:MLA_ROPE_DIM + GLA_GATE_RANK, :gw].set(gate_w2[0])
    wg = wg.at[MLA_ROPE_DIM + GLA_GATE_RANK:MLA_ROPE_DIM + 2 * GLA_GATE_RANK, gw:].set(gate_w2[1])
    gb = jnp.concatenate([gate_b[0], gate_b[1]])[None, :]
    q_p, k_p, v_p, la = _mla_prep(h, q_norm[None, :], wuq_l.astype(BF16), kv_norm[None, :],
                                  wukv_l.astype(BF16), wg.astype(BF16), gb, cos_r, sin_r, seq, tm=256)
    o_mla = _attention(q_p, k_p, v_p, batch=batch, seq=seq, kv_heads=MLA_HEADS, group=1,
                       dqk=MLA_QK_PAD, dv=MLA_V_DIM, q_off=0, k_off=0, v_off=0, tq=1024)
    o_gla = _gla(h, la, gla_norm.reshape(GLA_HEADS, 1, GLA_DV), batch, seq)
    return o_mla, o_gla


def _odd_mixer(xb, batch, seq, w_qkv, q_norm, k_norm, cos_g, sin_g):
    hd = GQA_HEAD_DIM
    scale = hd ** -0.5
    gains = jnp.concatenate([jnp.tile(q_norm * scale, GQA_HEADS), jnp.tile(k_norm, GQA_KV_HEADS),
                             jnp.ones((GQA_KV_HEADS * hd,), F32)]).reshape(-1, 1, 4 * hd)
    qkv = _qkv_proj(xb, w_qkv.astype(BF16), gains, cos_g, sin_g, seq, tm=1024)
    group = GQA_HEADS // GQA_KV_HEADS
    return _attention(qkv, qkv, qkv, batch=batch, seq=seq, kv_heads=GQA_KV_HEADS, group=group,
                      dqk=hd, dv=hd, q_off=0, k_off=GQA_HEADS, v_off=GQA_HEADS + GQA_KV_HEADS, tq=256)


def kernel(x, mix_w_in, mla_q_norm, mla_w_uq, mla_kv_norm, mla_w_ukv, gla_gate_w2, gla_gate_b,
           gla_out_norm, mix_w_out, gqa_w_qkv, gqa_q_norm, gqa_k_norm, gqa_w_out,
           moe_router, moe_w1, moe_w3, moe_w2, ln_mix_g, ln_mix_b, ln_ffn_g, ln_ffn_b):
    batch, seq, d = x.shape
    cos_r, sin_r = _rope_tables(seq, MLA_ROPE_DIM)
    cos_g, sin_g = _rope_tables(seq, GQA_HEAD_DIM)
    xf = x.reshape(batch * seq, d)
    xb = xf.astype(BF16)
    n_layers = moe_router.shape[0]
    for layer in range(n_layers):
        i = layer // 2
        wr_hi, wr_lo = _split_router(moe_router[layer])
        g_mix, b_mix = ln_mix_g[layer][None, :], ln_mix_b[layer][None, :]
        if layer % 2 == 0:
            o_mla, o_gla = _even_mixer(xb, batch, seq, mix_w_in[i], mla_q_norm[i], mla_w_uq[i],
                                       mla_kv_norm[i], mla_w_ukv[i], gla_gate_w2[i], gla_gate_b[i],
                                       gla_out_norm[i], cos_r, sin_r)
            w_out = mix_w_out[i].astype(BF16)
            n_mla = MLA_HEADS * MLA_V_DIM
            x1, xp1, logits_t = _proj_ln([o_mla, o_gla], [w_out[:n_mla], w_out[n_mla:]], xf,
                                         g_mix, b_mix, wr_hi, wr_lo, tm=512)
        else:
            o = _odd_mixer(xb, batch, seq, gqa_w_qkv[i], gqa_q_norm[i], gqa_k_norm[i], cos_g, sin_g)
            x1, xp1, logits_t = _proj_ln([o], [gqa_w_out[i].astype(BF16)], xf,
                                         g_mix, b_mix, wr_hi, wr_lo, tm=512)
        z = _moe(x1, xp1, logits_t, moe_w1, moe_w3, moe_w2, layer, batch, seq)
        xf, xb = _ln(z, ln_ffn_g[layer][None, :], ln_ffn_b[layer][None, :], tm=512)
    return xf.reshape(batch, seq, d)
```

```python
import functools

import jax
import jax.numpy as jnp
from jax import lax
from jax.experimental import pallas as pl
from jax.experimental.pallas import tpu as pltpu

F32 = jnp.float32
BF16 = jnp.bfloat16

GRID_W = 64
ROPE_THETA = 10000.0
LN_EPS = 1e-5
RMS_EPS = 1e-6
DEPTH = 4
DEEPNORM_ALPHA = (2.0 * DEPTH) ** 0.25

MLA_HEADS = 8
MLA_Q_RANK = 512
MLA_KV_RANK = 256
MLA_NOPE_DIM = 128
MLA_ROPE_DIM = 64
MLA_V_DIM = 128
MLA_QK_PAD = 256

GLA_HEADS = 4
GLA_DK = 128
GLA_DV = 256
GLA_GATE_RANK = 16
GLA_GATE_TAU = 16.0
GLA_CHUNK = 64
GLA_GROUP = 256
GLA_SCAN_UNROLL = 2

GQA_HEADS = 16
GQA_KV_HEADS = 4
GQA_HEAD_DIM = 128

N_EXPERTS = 16
CAPACITY_FACTOR = 2

LANES = 128
VMEM_LIMIT_BYTES = 56 * 1024 * 1024

LOG2_E = 1.4426950408889634
ATTN_SAFE_LOG2_RANGE = 60.0

H_CQ = 0
H_GQ = 512
H_GK = 1024
H_CKV = 1536
H_TAIL = 1792
H_GV = 2048
H_GR = 3072
H_WIDTH = 4096


def _cparams(sem):
    return pltpu.CompilerParams(dimension_semantics=sem, vmem_limit_bytes=VMEM_LIMIT_BYTES)


def _nt_dot(a, b):
    return lax.dot_general(a, b, (((1,), (1,)), ((), ())), preferred_element_type=F32)


def _tn_dot(a, b):
    return lax.dot_general(a, b, (((0,), (0,)), ((), ())), preferred_element_type=F32)


def _rope(x, cos, sin_signed):
    lane = lax.broadcasted_iota(jnp.int32, x.shape, 1)
    partner = jnp.where((lane & 1) == 0, pltpu.roll(x, LANES - 1, 1), pltpu.roll(x, 1, 1))
    return x * cos + partner * sin_signed


def _rms(x, gain):
    return x * lax.rsqrt(jnp.mean(x * x, axis=-1, keepdims=True) + RMS_EPS) * gain


def _layer_norm(z, g, b):
    mu = jnp.mean(z, axis=-1, keepdims=True)
    zc = z - mu
    var = jnp.mean(zc * zc, axis=-1, keepdims=True)
    return zc * lax.rsqrt(var + LN_EPS) * g + b


def _mm_kernel(x_ref, w_ref, o_ref):
    o_ref[...] = jnp.dot(x_ref[...], w_ref[...], preferred_element_type=F32).astype(o_ref.dtype)


def _matmul(x, w, out_dtype, tm, tn):
    m, k = x.shape
    n = w.shape[1]
    tm, tn = min(tm, m), min(tn, n)
    return pl.pallas_call(
        _mm_kernel,
        grid=(n // tn, m // tm),
        in_specs=[pl.BlockSpec((tm, k), lambda j, i: (i, 0)),
                  pl.BlockSpec((k, tn), lambda j, i: (0, j))],
        out_specs=pl.BlockSpec((tm, tn), lambda j, i: (i, j)),
        out_shape=jax.ShapeDtypeStruct((m, n), out_dtype),
        compiler_params=_cparams(("parallel", "parallel")),
        name="dense_matmul",
    )(x, w)


def _qkv_kernel(x_ref, w_ref, g_ref, ones_ref, cos_ref, sin_ref, o_ref, *, heads_per_tile):
    acc = jnp.dot(x_ref[...], w_ref[...], preferred_element_type=F32)
    sq = acc * acc
    sq_hi = sq.astype(BF16)
    sq_lo = (sq - sq_hi.astype(F32)).astype(BF16)
    ones_bd = ones_ref[...]
    ms = (jnp.dot(sq_hi, ones_bd, preferred_element_type=F32)
          + jnp.dot(sq_lo, ones_bd, preferred_element_type=F32)) * (1.0 / GQA_HEAD_DIM)
    y = acc * lax.rsqrt(ms + RMS_EPS) * g_ref[0:1, :]
    cos = cos_ref[...]
    sin = sin_ref[...]
    for hd in range(heads_per_tile):
        sl = slice(hd * GQA_HEAD_DIM, (hd + 1) * GQA_HEAD_DIM)
        yh = y[:, sl]
        rot = yh * cos + pltpu.roll(yh, GQA_HEAD_DIM // 2, 1) * sin
        o_ref[:, sl] = jnp.where(g_ref[1:2, sl] > 0.5, rot, acc[:, sl]).astype(o_ref.dtype)


def _qkv_proj(xb, w, gains, cos, sin, seq, tm):
    m, k = xb.shape
    n = w.shape[1]
    tn = 4 * GQA_HEAD_DIM
    tm = min(tm, seq)
    n_tiles = n // tn
    pos_blocks = seq // tm
    kern = functools.partial(_qkv_kernel, heads_per_tile=4)
    col_head = jnp.arange(tn, dtype=jnp.int32) // GQA_HEAD_DIM
    ones_bd = (col_head[:, None] == col_head[None, :]).astype(BF16)
    return pl.pallas_call(
        kern,
        grid=(n_tiles, m // tm),
        in_specs=[pl.BlockSpec((tm, k), lambda j, i: (i, 0)),
                  pl.BlockSpec((k, tn), lambda j, i: (0, j)),
                  pl.BlockSpec((None, 2, tn), lambda j, i: (j, 0, 0)),
                  pl.BlockSpec((tn, tn), lambda j, i: (0, 0)),
                  pl.BlockSpec((tm, LANES), lambda j, i: (i % pos_blocks, 0)),
                  pl.BlockSpec((tm, LANES), lambda j, i: (i % pos_blocks, 0))],
        out_specs=pl.BlockSpec((tm, tn), lambda j, i: (i, j)),
        out_shape=jax.ShapeDtypeStruct((m, n), BF16),
        compiler_params=_cparams(("parallel", "parallel")),
        name="gqa_qkv_proj",
    )(xb, w, gains, ones_bd, cos, sin)


def _attn_kernel(q_ref, k_ref, v_ref, o_ref, vx_s, kmax_s, *, group, dqk, dv, kc_single, kc_online):
    tq = q_ref.shape[0]
    seq = k_ref.shape[0]
    rows = group * tq

    @pl.when(pl.program_id(2) == 0)
    def _():
        lane = lax.broadcasted_iota(jnp.int32, (seq, LANES), 1)
        vx_s[:, :dv] = v_ref[...]
        vx_s[:, dv:] = jnp.where(lane == 0, 1.0, 0.0).astype(BF16)
        k32 = k_ref[...].astype(F32)
        k_sq = jnp.sum(k32 * k32, axis=1, keepdims=True)
        kmax_s[...] = jnp.max(k_sq, axis=0, keepdims=True)

    q = jnp.concatenate([q_ref[:, g * dqk:(g + 1) * dqk] for g in range(group)], axis=0)
    q32 = q.astype(F32)
    bound = jnp.sqrt(jnp.sum(q32 * q32, axis=1, keepdims=True) * kmax_s[...])
    single_pass = jnp.max(bound) <= ATTN_SAFE_LOG2_RANGE

    def write(acc):
        o = acc[:, :dv] / acc[:, dv:dv + 1]
        for g in range(group):
            o_ref[:, g * dv:(g + 1) * dv] = o[g * tq:(g + 1) * tq, :].astype(o_ref.dtype)

    @pl.when(single_pass)
    def _():
        kc = kc_single
        acc = jnp.zeros((rows, dv + LANES), F32)
        for c in range(seq // kc):
            p = jnp.exp2(_nt_dot(q, k_ref[c * kc:(c + 1) * kc, :]) - bound)
            acc = acc + jnp.dot(p.astype(BF16), vx_s[c * kc:(c + 1) * kc, :], preferred_element_type=F32)
        write(acc)

    @pl.when(jnp.logical_not(single_pass))
    def _():
        kc = kc_online
        m = jnp.full((rows, 1), -jnp.inf, F32)
        acc = jnp.zeros((rows, dv + LANES), F32)
        for c in range(seq // kc):
            s = _nt_dot(q, k_ref[c * kc:(c + 1) * kc, :])
            m_new = jnp.maximum(m, jnp.max(s, axis=1, keepdims=True))
            p = jnp.exp2(s - m_new)
            acc = acc * jnp.exp2(m - m_new) + jnp.dot(p.astype(BF16), vx_s[c * kc:(c + 1) * kc, :],
                                                      preferred_element_type=F32)
            m = m_new
        write(acc)


def _attention(q_arr, k_arr, v_arr, *, batch, seq, kv_heads, group, dqk, dv,
               q_off, k_off, v_off, tq):
    tq = min(tq, seq)
    nq = seq // tq
    kern = functools.partial(_attn_kernel, group=group, dqk=dqk, dv=dv,
                             kc_single=min(256, seq), kc_online=min(1024, seq))
    return pl.pallas_call(
        kern,
        grid=(batch, kv_heads, nq),
        in_specs=[pl.BlockSpec((tq, group * dqk), lambda b, h, i: (b * nq + i, q_off + h)),
                  pl.BlockSpec((seq, dqk), lambda b, h, i: (b, k_off + h)),
                  pl.BlockSpec((seq, dv), lambda b, h, i: (b, v_off + h))],
        out_specs=pl.BlockSpec((tq, group * dv), lambda b, h, i: (b * nq + i, h)),
        out_shape=jax.ShapeDtypeStruct((batch * seq, kv_heads * group * dv), BF16),
        scratch_shapes=[pltpu.VMEM((seq, dv + LANES), BF16), pltpu.VMEM((1, 1), F32)],
        compiler_params=_cparams(("arbitrary", "arbitrary", "arbitrary")),
        name="softmax_attention",
    )(q_arr, k_arr, v_arr)


def _mla_prep_kernel(cq_ref, ckv_ref, tail_ref, qn_ref, wuq_ref, kvn_ref, wukv_ref, wg_ref, gb_ref,
                     cos_ref, sin_ref, q_ref, k_ref, v_ref, la_ref):
    cos = cos_ref[...]
    sin = sin_ref[...]
    scale = (MLA_NOPE_DIM + MLA_ROPE_DIM) ** -0.5 * LOG2_E
    q = jnp.dot(_rms(cq_ref[...], qn_ref[...]).astype(BF16), wuq_ref[...],
                preferred_element_type=F32) * scale
    kv = jnp.dot(_rms(ckv_ref[...], kvn_ref[...]).astype(BF16), wukv_ref[...],
                 preferred_element_type=F32)
    tail = tail_ref[...]
    lane = lax.broadcasted_iota(jnp.int32, tail.shape, 1)
    k_pe = jnp.where(lane < MLA_ROPE_DIM, _rope(tail, cos, sin), 0.0).astype(BF16)
    for hd in range(MLA_HEADS):
        c0 = hd * MLA_QK_PAD
        q_ref[:, c0:c0 + LANES] = q[:, c0:c0 + LANES].astype(BF16)
        q_ref[:, c0 + LANES:c0 + 2 * LANES] = _rope(q[:, c0 + LANES:c0 + 2 * LANES], cos, sin).astype(BF16)
        k_ref[:, c0:c0 + LANES] = kv[:, hd * LANES:(hd + 1) * LANES].astype(BF16)
        k_ref[:, c0 + LANES:c0 + 2 * LANES] = k_pe
    nv = MLA_HEADS * MLA_V_DIM
    v_ref[...] = kv[:, nv:].astype(BF16)
    gate = jnp.dot(tail.astype(BF16), wg_ref[...], preferred_element_type=F32) + gb_ref[...]
    log_sig = jnp.minimum(gate, 0.0) - jnp.log1p(jnp.exp(-jnp.abs(gate)))
    la_ref[...] = log_sig / GLA_GATE_TAU


def _mla_prep(h, qn, wuq, kvn, wukv, wg, gb, cos, sin, seq, tm):
    m = h.shape[0]
    tm = min(tm, seq)
    pos_blocks = seq // tm
    qw = MLA_HEADS * MLA_QK_PAD
    vw = MLA_HEADS * MLA_V_DIM
    gw = 2 * GLA_HEADS * GLA_DK
    const = lambda i: (0, 0)
    return pl.pallas_call(
        _mla_prep_kernel,
        grid=(m // tm,),
        in_specs=[pl.BlockSpec((tm, MLA_Q_RANK), lambda i: (i, H_CQ // MLA_Q_RANK)),
                  pl.BlockSpec((tm, MLA_KV_RANK), lambda i: (i, H_CKV // MLA_KV_RANK)),
                  pl.BlockSpec((tm, LANES), lambda i: (i, H_TAIL // LANES)),
                  pl.BlockSpec((1, MLA_Q_RANK), const),
                  pl.BlockSpec((MLA_Q_RANK, qw), const),
                  pl.BlockSpec((1, MLA_KV_RANK), const),
                  pl.BlockSpec((MLA_KV_RANK, 2 * vw), const),
                  pl.BlockSpec((LANES, gw), const),
                  pl.BlockSpec((1, gw), const),
                  pl.BlockSpec((tm, LANES), lambda i: (i % pos_blocks, 0)),
                  pl.BlockSpec((tm, LANES), lambda i: (i % pos_blocks, 0))],
        out_specs=[pl.BlockSpec((tm, qw), lambda i: (i, 0)),
                   pl.BlockSpec((tm, qw), lambda i: (i, 0)),
                   pl.BlockSpec((tm, vw), lambda i: (i, 0)),
                   pl.BlockSpec((tm, gw), lambda i: (i, 0))],
        out_shape=[jax.ShapeDtypeStruct((m, qw), BF16),
                   jax.ShapeDtypeStruct((m, qw), BF16),
                   jax.ShapeDtypeStruct((m, vw), BF16),
                   jax.ShapeDtypeStruct((m, gw), F32)],
        compiler_params=_cparams(("parallel",)),
        name="mla_prep",
    )(h, h, h, qn, wuq, kvn, wukv, wg, gb, cos, sin)


def _split3(x):
    hi = x.astype(BF16)
    r1 = x - hi.astype(F32)
    mid = r1.astype(BF16)
    lo = (r1 - mid.astype(F32)).astype(BF16)
    return hi, mid, lo


def _gla_kernel(q_ref, k_ref, v_ref, gr_ref, laf_ref, lab_ref, gn_ref, o_ref,
                of_s, ob_s, qf_s, qb_s, kf_s, kb_s, df_s, db_s, stf_s, stb_s):
    seq = q_ref.shape[0]
    c = GLA_CHUNK
    r = min(GLA_GROUP, seq)
    cpg = r // c
    n_groups = seq // r
    n_chunks = seq // c
    scale = GLA_DK ** -0.5

    row = lax.broadcasted_iota(jnp.int32, (r, r), 0)
    col = lax.broadcasted_iota(jnp.int32, (r, r), 1)
    same = (row // c) == (col // c)
    tril = same & (col <= row)
    triu = same & (col >= row)
    tril_b = tril.astype(F32).astype(BF16)
    triu_b = triu.astype(F32).astype(BF16)

    def group_body(gi, carry):
        r0 = pl.multiple_of(gi * r, r)
        rows = pl.ds(r0, r)
        q = q_ref[rows, :] * scale
        k = k_ref[rows, :]
        v = v_ref[rows, :].astype(BF16)
        for la_ref, mask, mask_b, edge, o_s, q_s, k_s, d_s in (
                (laf_ref, tril, tril_b, c - 1, of_s, qf_s, kf_s, df_s),
                (lab_ref, triu, triu_b, 0, ob_s, qb_s, kb_s, db_s)):
            hi, mid, lo = _split3(la_ref[rows, :])
            b = (jnp.dot(mask_b, hi, preferred_element_type=F32)
                 + jnp.dot(mask_b, mid, preferred_element_type=F32)
                 + jnp.dot(mask_b, lo, preferred_element_type=F32))
            b_edge = jnp.concatenate(
                [jnp.broadcast_to(b[ci * c + edge:ci * c + edge + 1, :], (c, GLA_DK)) for ci in range(cpg)],
                axis=0)
            q_in = (q * jnp.exp(b)).astype(BF16)
            k_in = (k * jnp.exp(-b)).astype(BF16)
            k_st = (k * jnp.exp(b_edge - b)).astype(BF16)
            att = jnp.where(mask, _nt_dot(q_in, k_in), 0.0)
            o_s[rows, :] = jnp.dot(att.astype(BF16), v, preferred_element_type=F32)
            q_s[rows, :] = q_in
            k_s[rows, :] = k_st
            for ci in range(cpg):
                d_s[pl.ds(gi * cpg + ci, 1), :] = jnp.exp(b[ci * c + edge:ci * c + edge + 1, :])
        return carry

    lax.fori_loop(0, n_groups, group_body, 0)

    stf_s[...] = jnp.zeros_like(stf_s)
    stb_s[...] = jnp.zeros_like(stb_s)

    def chunk_body(i, carry):
        for u in range(GLA_SCAN_UNROLL):
            fwd_n = i * GLA_SCAN_UNROLL + u
            for n, o_s, q_s, k_s, d_s, st_s in ((fwd_n, of_s, qf_s, kf_s, df_s, stf_s),
                                                (n_chunks - 1 - fwd_n, ob_s, qb_s, kb_s, db_s, stb_s)):
                rows = pl.ds(pl.multiple_of(n * c, c), c)
                st = st_s[...]
                o_s[rows, :] += _nt_dot(q_s[rows, :], st.astype(BF16))
                ds = _tn_dot(v_ref[rows, :].astype(BF16), k_s[rows, :])
                st_s[...] = d_s[pl.ds(n, 1), :] * st + ds
        return carry

    lax.fori_loop(0, n_chunks // GLA_SCAN_UNROLL, chunk_body, 0)

    gain = gn_ref[...]

    def out_body(gi, carry):
        rows = pl.ds(pl.multiple_of(gi * r, r), r)
        o = _rms(of_s[rows, :] + ob_s[rows, :], gain)
        gr = gr_ref[rows, :]
        o_ref[rows, :] = (o * (gr * jax.nn.sigmoid(gr))).astype(o_ref.dtype)
        return carry

    lax.fori_loop(0, n_groups, out_body, 0)


def _gla(h, la, gn, batch, seq):
    dk, dv, nh = GLA_DK, GLA_DV, GLA_HEADS
    n_chunks = seq // GLA_CHUNK
    return pl.pallas_call(
        _gla_kernel,
        grid=(batch, nh),
        in_specs=[pl.BlockSpec((seq, dk), lambda b, hd: (b, H_GQ // dk + hd)),
                  pl.BlockSpec((seq, dk), lambda b, hd: (b, H_GK // dk + hd)),
                  pl.BlockSpec((seq, dv), lambda b, hd: (b, H_GV // dv + hd)),
                  pl.BlockSpec((seq, dv), lambda b, hd: (b, H_GR // dv + hd)),
                  pl.BlockSpec((seq, dk), lambda b, hd: (b, hd)),
                  pl.BlockSpec((seq, dk), lambda b, hd: (b, nh + hd)),
                  pl.BlockSpec((None, 1, dv), lambda b, hd: (hd, 0, 0))],
        out_specs=pl.BlockSpec((seq, dv), lambda b, hd: (b, hd)),
        out_shape=jax.ShapeDtypeStruct((batch * seq, nh * dv), BF16),
        scratch_shapes=[pltpu.VMEM((seq, dv), F32), pltpu.VMEM((seq, dv), F32),
                        pltpu.VMEM((seq, dk), BF16), pltpu.VMEM((seq, dk), BF16),
                        pltpu.VMEM((seq, dk), BF16), pltpu.VMEM((seq, dk), BF16),
                        pltpu.VMEM((n_chunks, dk), F32), pltpu.VMEM((n_chunks, dk), F32),
                        pltpu.VMEM((dv, dk), F32), pltpu.VMEM((dv, dk), F32)],
        compiler_params=_cparams(("arbitrary", "arbitrary")),
        name="gla_bidirectional",
    )(h, h, h, h, la, la, gn)


def _proj_ln_kernel(*refs, n_in):
    a_refs = refs[:n_in]
    w_refs = refs[n_in:2 * n_in]
    x_ref, g_ref, b_ref, wrh_ref, wrl_ref, xo_ref, xp_ref, lg_ref = refs[2 * n_in:]
    mix = jnp.dot(a_refs[0][...], w_refs[0][...], preferred_element_type=F32)
    for a_ref, w_ref in zip(a_refs[1:], w_refs[1:]):
        mix = mix + jnp.dot(a_ref[...], w_ref[...], preferred_element_type=F32)
    y = _layer_norm(DEEPNORM_ALPHA * x_ref[...] + mix, g_ref[...], b_ref[...])
    xo_ref[...] = y
    y_hi = y.astype(BF16)
    y_hi32 = y_hi.astype(F32)
    half = y.shape[1] // 2
    bits = lax.bitcast_convert_type(y_hi32, jnp.int32)
    xp_ref[...] = lax.shift_right_logical(bits[:, :half], 16) | bits[:, half:]
    y_lo = (y - y_hi32).astype(BF16)
    wrh = wrh_ref[...]
    lg_ref[...] = _nt_dot(wrh, y_hi) + _nt_dot(wrh, y_lo) + _nt_dot(wrl_ref[...], y_hi)


def _proj_ln(acts, weights, x, g, b, wr_hi, wr_lo, tm):
    m, d = x.shape
    tm = min(tm, m)
    n_in = len(acts)
    n_e = wr_hi.shape[0]
    const = lambda i: (0, 0)
    in_specs = ([pl.BlockSpec((tm, a.shape[1]), lambda i: (i, 0)) for a in acts]
                + [pl.BlockSpec(w.shape, const) for w in weights]
                + [pl.BlockSpec((tm, d), lambda i: (i, 0)),
                   pl.BlockSpec((1, d), const), pl.BlockSpec((1, d), const),
                   pl.BlockSpec((n_e, d), const), pl.BlockSpec((n_e, d), const)])
    return pl.pallas_call(
        functools.partial(_proj_ln_kernel, n_in=n_in),
        grid=(m // tm,),
        in_specs=in_specs,
        out_specs=[pl.BlockSpec((tm, d), lambda i: (i, 0)),
                   pl.BlockSpec((tm, d // 2), lambda i: (i, 0)),
                   pl.BlockSpec((n_e, tm), lambda i: (0, i))],
        out_shape=[jax.ShapeDtypeStruct((m, d), F32),
                   jax.ShapeDtypeStruct((m, d // 2), jnp.int32),
                   jax.ShapeDtypeStruct((n_e, m), F32)],
        compiler_params=_cparams(("parallel",)),
        name="out_proj_layernorm_router",
    )(*acts, *weights, x, g, b, wr_hi, wr_lo)


def _ln_kernel(z_ref, g_ref, b_ref, xo_ref, xb_ref):
    y = _layer_norm(z_ref[...], g_ref[...], b_ref[...])
    xo_ref[...] = y
    xb_ref[...] = y.astype(BF16)


def _ln(z, g, b, tm):
    m, d = z.shape
    tm = min(tm, m)
    const = lambda i: (0, 0)
    return pl.pallas_call(
        _ln_kernel,
        grid=(m // tm,),
        in_specs=[pl.BlockSpec((tm, d), lambda i: (i, 0)),
                  pl.BlockSpec((1, d), const), pl.BlockSpec((1, d), const)],
        out_specs=[pl.BlockSpec((tm, d), lambda i: (i, 0)),
                   pl.BlockSpec((tm, d), lambda i: (i, 0))],
        out_shape=[jax.ShapeDtypeStruct((m, d), F32), jax.ShapeDtypeStruct((m, d), BF16)],
        compiler_params=_cparams(("parallel",)),
        name="layernorm",
    )(z, g, b)


def _route_kernel(lg_ref, idx_ref, gate_ref, aff_s, cum_s, start_s, *, cap):
    lg = lg_ref[...]
    ex = jnp.exp(lg - jnp.max(lg, axis=0, keepdims=True))
    aff = ex / jnp.sum(ex, axis=0, keepdims=True)
    bits = lax.bitcast_convert_type(aff, jnp.int32)
    n_e, seq = lg.shape
    nblk = seq // LANES
    capf = float(cap)

    def bisect(_, carry):
        lo, hi = carry
        mid = lo + ((hi - lo + 1) >> 1)
        cnt = jnp.sum(jnp.where(bits >= mid, 1.0, 0.0), axis=1, keepdims=True)
        ok = cnt >= capf
        return jnp.where(ok, mid, lo), jnp.where(ok, hi, mid - 1)

    lo0 = jnp.zeros((n_e, 1), jnp.int32)
    hi0 = jnp.full((n_e, 1), 0x7F800000, jnp.int32)
    thr, _ = lax.fori_loop(0, 32, bisect, (lo0, hi0))

    gt = bits > thr
    eq = bits == thr
    need = capf - jnp.sum(jnp.where(gt, 1.0, 0.0), axis=1, keepdims=True)

    r_i = lax.broadcasted_iota(jnp.int32, (LANES, LANES), 0)
    c_i = lax.broadcasted_iota(jnp.int32, (LANES, LANES), 1)
    upper = (r_i <= c_i).astype(F32).astype(BF16)
    eq_off = jnp.zeros((n_e, 1), F32)
    sel_off = jnp.zeros((n_e, 1), F32)
    for j in range(nblk):
        blk = slice(j * LANES, (j + 1) * LANES)
        eq01 = jnp.where(eq[:, blk], 1.0, 0.0)
        eq_cum = jnp.dot(eq01.astype(BF16), upper, preferred_element_type=F32) + eq_off
        eq_off = eq_cum[:, LANES - 1:LANES]
        tie_rank = eq_cum - eq01
        sel01 = jnp.where(gt[:, blk] | (eq[:, blk] & (tie_rank < need)), 1.0, 0.0)
        sel_cum = jnp.dot(sel01.astype(BF16), upper, preferred_element_type=F32) + sel_off
        rows = slice(j * n_e, (j + 1) * n_e)
        cum_s[rows, :] = sel_cum
        start_s[rows, :] = jnp.broadcast_to(sel_off, (n_e, LANES))
        aff_s[rows, :] = aff[:, blk]
        sel_off = sel_cum[:, LANES - 1:LANES]

    slot_row = lax.broadcasted_iota(jnp.int32, (1, cap), 1).astype(F32)
    slot_col = lax.broadcasted_iota(jnp.int32, (cap, 1), 0).astype(F32)
    lane = lax.broadcasted_iota(jnp.int32, (cap, LANES), 1).astype(F32)
    blk_id = lax.broadcasted_iota(jnp.int32, (nblk, LANES), 0).astype(F32).astype(BF16)

    def extract(e, carry):
        rows = pl.ds(e, nblk, stride=n_e)
        cum_e = cum_s[rows, :]
        start_e = start_s[rows, :][:, 0:1]
        end_e = cum_e[:, LANES - 1:LANES]
        in_blk = jnp.where((start_e <= slot_row) & (slot_row < end_e), 1.0, 0.0).astype(BF16)
        cum_hi = jnp.floor(cum_e * (1.0 / 32.0))
        cum_lo = cum_e - 32.0 * cum_hi
        g = 32.0 * _tn_dot(in_blk, cum_hi.astype(BF16)) + _tn_dot(in_blk, cum_lo.astype(BF16))
        blk_of_slot = _tn_dot(in_blk, blk_id)[:, 0:1]
        pos = jnp.sum(jnp.where(g <= slot_col, 1.0, 0.0), axis=1, keepdims=True)
        a_hi, a_mid, a_lo = _split3(aff_s[rows, :])
        a = (_tn_dot(in_blk, a_hi) + _tn_dot(in_blk, a_mid)) + _tn_dot(in_blk, a_lo)
        gate = jnp.sum(jnp.where(lane == pos, a, 0.0), axis=1, keepdims=True)
        out_rows = pl.ds(pl.multiple_of(e * cap, cap), cap)
        idx_ref[out_rows, :] = (float(LANES) * blk_of_slot + pos).astype(jnp.int32)
        gate_ref[out_rows, :] = gate
        return carry

    lax.fori_loop(0, n_e, extract, 0)


def _route(logits_t, batch, seq, cap):
    n_e = logits_t.shape[0]
    kern = functools.partial(_route_kernel, cap=cap)
    scratch_rows = (seq // LANES) * n_e
    return pl.pallas_call(
        kern,
        grid=(batch,),
        in_specs=[pl.BlockSpec((n_e, seq), lambda b: (0, b))],
        out_specs=[pl.BlockSpec((n_e * cap, 1), lambda b: (b, 0)),
                   pl.BlockSpec((n_e * cap, 1), lambda b: (b, 0))],
        out_shape=[jax.ShapeDtypeStruct((batch * n_e * cap, 1), jnp.int32),
                   jax.ShapeDtypeStruct((batch * n_e * cap, 1), F32)],
        scratch_shapes=[pltpu.VMEM((scratch_rows, LANES), F32), pltpu.VMEM((scratch_rows, LANES), F32),
                        pltpu.VMEM((scratch_rows, LANES), F32)],
        compiler_params=_cparams(("parallel",)),
        name="expert_choice_route",
    )(logits_t)


GATHER_UNROLL = 32


def _ffn_kernel(idx_cur, idx_nxt, xp_hbm, w1_ref, w3_ref, w2_ref, gate_ref, o_ref,
                land, x_lo, x_hi, sem, *, cap, seq, bpt, issue_steps, unpack_rows):
    m_tiles = pl.num_programs(1)
    f = pl.program_id(2)
    t = pl.program_id(0) * m_tiles + pl.program_id(1)
    n_tiles = pl.num_programs(0) * m_tiles
    slot = t % 2
    tm = bpt * cap
    half = x_lo.shape[1]

    def row_copy(dst_slot, r, src_row):
        return pltpu.make_async_copy(xp_hbm.at[pl.ds(src_row, 1), :],
                                     land.at[dst_slot, pl.ds(r, 1), :], sem.at[dst_slot])

    def issue(idx_ref, tile, dst_slot, bb, s0, n):
        src_base = ((tile % m_tiles) * bpt + bb) * seq
        dst_base = bb * cap

        def body(i, carry):
            for u in range(GATHER_UNROLL):
                s = s0 + i * GATHER_UNROLL + u
                row_copy(dst_slot, dst_base + s, src_base + idx_ref[bb, 0, s]).start()
            return carry

        lax.fori_loop(0, n // GATHER_UNROLL, body, 0)

    @pl.when((t == 0) & (f == 0))
    def _():
        for bb in range(bpt):
            issue(idx_cur, t, slot, bb, 0, cap)

    @pl.when(f == 0)
    def _():
        pltpu.make_async_copy(xp_hbm.at[pl.ds(0, tm), :], land.at[slot], sem.at[slot]).wait()

        def unpack(i, carry):
            rows = pl.ds(pl.multiple_of(i * unpack_rows, unpack_rows), unpack_rows)
            w = land[slot, rows, :]
            x_lo[rows, :] = lax.bitcast_convert_type(w << 16, F32).astype(BF16)
            x_hi[rows, :] = lax.bitcast_convert_type(w & jnp.int32(-65536), F32).astype(BF16)
            return carry

        lax.fori_loop(0, tm // unpack_rows, unpack, 0)
        o_ref[...] = jnp.zeros_like(o_ref)

    per_batch = issue_steps // bpt
    n = cap // per_batch
    for bb in range(bpt):
        @pl.when((f >= bb * per_batch) & (f < (bb + 1) * per_batch) & (t + 1 < n_tiles))
        def _(bb=bb):
            issue(idx_nxt, t + 1, 1 - slot, bb, (f - bb * per_batch) * n, n)

    w1 = w1_ref[...].astype(BF16)
    w3 = w3_ref[...].astype(BF16)
    lo, hi = x_lo[...], x_hi[...]
    h1 = (jnp.dot(lo, w1[:half], preferred_element_type=F32)
          + jnp.dot(hi, w1[half:], preferred_element_type=F32))
    h3 = (jnp.dot(lo, w3[:half], preferred_element_type=F32)
          + jnp.dot(hi, w3[half:], preferred_element_type=F32))
    hdn = (h1 * jax.nn.sigmoid(h1)) * h3
    part = jnp.dot(hdn.astype(BF16), w2_ref[...].astype(BF16), preferred_element_type=F32)
    is_last = f == pl.num_programs(2) - 1
    gate = jnp.where(is_last, gate_ref[...], 1.0)
    o_ref[...] = (o_ref[...] + part) * gate


def _ffn(idx_eb, xp, w1, w3, w2, layer, gate_col, batch, seq, cap, tf):
    _, n_e, d, ff = w1.shape
    bpt = min(2, batch)
    tm = bpt * cap
    mt = batch // bpt
    tf = min(tf, ff // bpt)
    nf = ff // tf
    assert nf >= 2 and ff % tf == 0
    issue_steps = bpt * min(2, nf // bpt)
    assert cap % (issue_steps // bpt * GATHER_UNROLL) == 0
    n_tiles = n_e * mt
    kern = functools.partial(_ffn_kernel, cap=cap, seq=seq, bpt=bpt, issue_steps=issue_steps,
                             unpack_rows=min(256, tm))
    smem_idx = lambda fn: pl.BlockSpec((bpt, 1, cap), fn, memory_space=pltpu.SMEM)
    return pl.pallas_call(
        kern,
        grid=(n_e, mt, nf),
        in_specs=[smem_idx(lambda e, m, f: (e * mt + m, 0, 0)),
                  smem_idx(lambda e, m, f: (jnp.minimum(e * mt + m + 1, n_tiles - 1), 0, 0)),
                  pl.BlockSpec(memory_space=pl.ANY),
                  pl.BlockSpec((None, None, d, tf), lambda e, m, f: (layer, e, 0, f)),
                  pl.BlockSpec((None, None, d, tf), lambda e, m, f: (layer, e, 0, f)),
                  pl.BlockSpec((None, None, tf, d), lambda e, m, f: (layer, e, f, 0)),
                  pl.BlockSpec((tm, 1), lambda e, m, f: (e * mt + m, 0))],
        out_specs=pl.BlockSpec((tm, d), lambda e, m, f: (e * mt + m, 0)),
        out_shape=jax.ShapeDtypeStruct((n_e * batch * cap, d), F32),
        scratch_shapes=[pltpu.VMEM((2, tm, d // 2), jnp.int32),
                        pltpu.VMEM((tm, d // 2), BF16), pltpu.VMEM((tm, d // 2), BF16),
                        pltpu.SemaphoreType.DMA((2,))],
        compiler_params=_cparams(("arbitrary", "arbitrary", "arbitrary")),
        name="moe_expert_ffn",
    )(idx_eb, idx_eb, xp, w1, w3, w2, gate_col)


COMBINE_UNROLL = 8


def _combine_kernel(idx_ref, y_ref, x_hbm, z_hbm, acc, sem, *, cap, seq, n_e, row_block):
    b = pl.program_id(0)
    e = pl.program_id(1)

    @pl.when(e == 0)
    def _():
        cp = pltpu.make_async_copy(x_hbm.at[pl.ds(b * seq, seq), :], acc, sem)
        cp.start()
        cp.wait()

        def scale(i, carry):
            rows = pl.ds(pl.multiple_of(i * row_block, row_block), row_block)
            acc[rows, :] = acc[rows, :] * DEEPNORM_ALPHA
            return carry

        lax.fori_loop(0, seq // row_block, scale, 0)

    def add_rows(i, carry):
        s0 = i * COMBINE_UNROLL
        toks = [idx_ref[0, 0, s0 + u] for u in range(COMBINE_UNROLL)]
        sums = [acc[pl.ds(toks[u], 1), :] + y_ref[pl.ds(s0 + u, 1), :] for u in range(COMBINE_UNROLL)]
        for u in range(COMBINE_UNROLL):
            acc[pl.ds(toks[u], 1), :] = sums[u]
        return carry

    lax.fori_loop(0, cap // COMBINE_UNROLL, add_rows, 0)

    @pl.when(e == n_e - 1)
    def _():
        cp = pltpu.make_async_copy(acc, z_hbm.at[pl.ds(b * seq, seq), :], sem)
        cp.start()
        cp.wait()


def _combine(idx_be, y, x, batch, seq, cap, n_e):
    d = x.shape[1]
    kern = functools.partial(_combine_kernel, cap=cap, seq=seq, n_e=n_e, row_block=min(256, seq))
    return pl.pallas_call(
        kern,
        grid=(batch, n_e),
        in_specs=[pl.BlockSpec((1, 1, cap), lambda b, e: (b * n_e + e, 0, 0), memory_space=pltpu.SMEM),
                  pl.BlockSpec((cap, d), lambda b, e: (e * batch + b, 0)),
                  pl.BlockSpec(memory_space=pl.ANY)],
        out_specs=pl.BlockSpec(memory_space=pl.ANY),
        out_shape=jax.ShapeDtypeStruct((batch * seq, d), F32),
        scratch_shapes=[pltpu.VMEM((seq, d), F32), pltpu.SemaphoreType.DMA(())],
        compiler_params=_cparams(("arbitrary", "arbitrary")),
        name="moe_combine",
    )(idx_be, y, x)


def _rope_tables(seq, rot_dim, split_halves=False):
    rows = seq // GRID_W
    row = jnp.repeat(jnp.arange(rows, dtype=jnp.int32), GRID_W).astype(F32)
    col = jnp.tile(jnp.arange(GRID_W, dtype=jnp.int32), rows).astype(F32)
    half = rot_dim // 2
    inv_freq = ROPE_THETA ** (-jnp.arange(0, half, 2, dtype=F32) / half)
    ang = jnp.concatenate([row[:, None] * inv_freq, col[:, None] * inv_freq], axis=-1)
    cos, sin = jnp.cos(ang), jnp.sin(ang)
    if split_halves:
        cos2 = jnp.concatenate([cos, cos], axis=1)
        sin2 = jnp.concatenate([-sin, sin], axis=1)
    else:
        cos2 = jnp.repeat(cos, 2, axis=1)
        sin2 = jnp.stack([-sin, sin], axis=-1).reshape(seq, rot_dim)
    pad = LANES - rot_dim
    if pad:
        cos2 = jnp.concatenate([cos2, jnp.ones((seq, pad), F32)], axis=1)
        sin2 = jnp.concatenate([sin2, jnp.zeros((seq, pad), F32)], axis=1)
    return cos2, sin2


def _moe(x1, xp1, logits_t, w1, w3, w2, layer, batch, seq):
    n_e = w1.shape[1]
    cap = CAPACITY_FACTOR * seq // n_e
    idx_col, gate_col = _route(logits_t, batch, seq, cap)
    idx_be = idx_col.reshape(batch * n_e, 1, cap)
    idx_eb = idx_col.reshape(batch, n_e, cap).transpose(1, 0, 2).reshape(n_e * batch, 1, cap)
    gate_eb = gate_col.reshape(batch, n_e, cap).transpose(1, 0, 2).reshape(n_e * batch * cap, 1)
    y = _ffn(idx_eb, xp1, w1, w3, w2, layer, gate_eb, batch, seq, cap, tf=256)
    return _combine(idx_be, y, x1, batch, seq, cap, n_e)


def _split_router(w_router):
    wr = w_router.T
    hi = wr.astype(BF16)
    lo = (wr - hi.astype(F32)).astype(BF16)
    return hi, lo


def _even_mixer(xb, batch, seq, w_in, q_norm, w_uq, kv_norm, w_ukv, gate_w2, gate_b, gla_norm,
                cos_r, sin_r):
    d = w_in.shape[0]
    o_cq, o_ckv, o_kr = 0, MLA_Q_RANK, MLA_Q_RANK + MLA_KV_RANK
    o_gq = o_kr + MLA_ROPE_DIM
    o_gk = o_gq + GLA_HEADS * GLA_DK
    o_gv = o_gk + GLA_HEADS * GLA_DK
    o_gr = o_gv + GLA_HEADS * GLA_DV
    o_lat = o_gr + GLA_HEADS * GLA_DV
    o_end = o_lat + 2 * GLA_GATE_RANK
    zeros = lambda n: jnp.zeros((d, n), w_in.dtype)
    tail_pad = LANES - MLA_ROPE_DIM - 2 * GLA_GATE_RANK
    w_in_l = jnp.concatenate([
        w_in[:, o_cq:o_ckv], w_in[:, o_gq:o_gk], w_in[:, o_gk:o_gv], w_in[:, o_ckv:o_kr],
        w_in[:, o_kr:o_gq], w_in[:, o_lat:o_end], zeros(tail_pad), zeros(H_GV - H_TAIL - LANES),
        w_in[:, o_gv:o_gr], w_in[:, o_gr:o_lat]], axis=1).astype(BF16)
    h = _matmul(xb, w_in_l, F32, tm=1024, tn=1024)

    qk = MLA_NOPE_DIM + MLA_ROPE_DIM
    wuq_l = jnp.pad(w_uq.reshape(MLA_Q_RANK, MLA_HEADS, qk),
                    ((0, 0), (0, 0), (0, MLA_QK_PAD - qk))).reshape(MLA_Q_RANK, MLA_HEADS * MLA_QK_PAD)
    wukv3 = w_ukv.reshape(MLA_KV_RANK, MLA_HEADS, MLA_NOPE_DIM + MLA_V_DIM)
    wukv_l = jnp.concatenate([wukv3[:, :, :MLA_NOPE_DIM].reshape(MLA_KV_RANK, -1),
                              wukv3[:, :, MLA_NOPE_DIM:].reshape(MLA_KV_RANK, -1)], axis=1)
    gw = GLA_HEADS * GLA_DK
    wg = jnp.zeros((LANES, 2 * gw), F32)
    wg = wg.at[MLA_ROPE_DIM:MLA_ROPE_DIM + GLA_GATE_RANK, :gw].set(gate_w2[0])
    wg = wg.at[MLA_ROPE_DIM + GLA_GATE_RANK:MLA_ROPE_DIM + 2 * GLA_GATE_RANK, gw:].set(gate_w2[1])
    gb = jnp.concatenate([gate_b[0], gate_b[1]])[None, :]
    q_p, k_p, v_p, la = _mla_prep(h, q_norm[None, :], wuq_l.astype(BF16), kv_norm[None, :],
                                  wukv_l.astype(BF16), wg.astype(BF16), gb, cos_r, sin_r, seq, tm=256)
    o_mla = _attention(q_p, k_p, v_p, batch=batch, seq=seq, kv_heads=MLA_HEADS, group=1,
                       dqk=MLA_QK_PAD, dv=MLA_V_DIM, q_off=0, k_off=0, v_off=0, tq=2048)
    o_gla = _gla(h, la, gla_norm.reshape(GLA_HEADS, 1, GLA_DV), batch, seq)
    return o_mla, o_gla


def _odd_mixer(xb, batch, seq, w_qkv, q_norm, k_norm, cos_g, sin_g):
    hd = GQA_HEAD_DIM
    scale = hd ** -0.5 * LOG2_E
    n_qk, n_v = (GQA_HEADS + GQA_KV_HEADS) * hd, GQA_KV_HEADS * hd
    split = jnp.concatenate([jnp.arange(0, hd, 2), jnp.arange(1, hd, 2)])
    col_perm = jnp.concatenate([(jnp.arange(n_qk // hd)[:, None] * hd + split[None, :]).reshape(-1),
                                n_qk + jnp.arange(n_v)])
    gain_row = jnp.concatenate([jnp.tile(q_norm[split] * scale, GQA_HEADS), jnp.tile(k_norm[split], GQA_KV_HEADS),
                                jnp.ones((n_v,), F32)]).reshape(-1, 1, 4 * hd)
    normed_row = jnp.concatenate([jnp.ones((n_qk,), F32), jnp.zeros((n_v,), F32)]).reshape(-1, 1, 4 * hd)
    gains = jnp.concatenate([gain_row, normed_row], axis=1)
    qkv = _qkv_proj(xb, w_qkv[:, col_perm].astype(BF16), gains, cos_g, sin_g, seq, tm=1024)
    group = GQA_HEADS // GQA_KV_HEADS
    return _attention(qkv, qkv, qkv, batch=batch, seq=seq, kv_heads=GQA_KV_HEADS, group=group,
                      dqk=hd, dv=hd, q_off=0, k_off=GQA_HEADS, v_off=GQA_HEADS + GQA_KV_HEADS, tq=512)


def kernel(x, mix_w_in, mla_q_norm, mla_w_uq, mla_kv_norm, mla_w_ukv, gla_gate_w2, gla_gate_b,
           gla_out_norm, mix_w_out, gqa_w_qkv, gqa_q_norm, gqa_k_norm, gqa_w_out,
           moe_router, moe_w1, moe_w3, moe_w2, ln_mix_g, ln_mix_b, ln_ffn_g, ln_ffn_b):
    batch, seq, d = x.shape
    cos_r, sin_r = _rope_tables(seq, MLA_ROPE_DIM)
    cos_g, sin_g = _rope_tables(seq, GQA_HEAD_DIM, split_halves=True)
    xf = x.reshape(batch * seq, d)
    xb = xf.astype(BF16)
    n_layers = moe_router.shape[0]
    for layer in range(n_layers):
        i = layer // 2
        wr_hi, wr_lo = _split_router(moe_router[layer])
        g_mix, b_mix = ln_mix_g[layer][None, :], ln_mix_b[layer][None, :]
        if layer % 2 == 0:
            o_mla, o_gla = _even_mixer(xb, batch, seq, mix_w_in[i], mla_q_norm[i], mla_w_uq[i],
                                       mla_kv_norm[i], mla_w_ukv[i], gla_gate_w2[i], gla_gate_b[i],
                                       gla_out_norm[i], cos_r, sin_r)
            w_out = mix_w_out[i].astype(BF16)
            n_mla = MLA_HEADS * MLA_V_DIM
            x1, xp1, logits_t = _proj_ln([o_mla, o_gla], [w_out[:n_mla], w_out[n_mla:]], xf,
                                         g_mix, b_mix, wr_hi, wr_lo, tm=512)
        else:
            o = _odd_mixer(xb, batch, seq, gqa_w_qkv[i], gqa_q_norm[i], gqa_k_norm[i], cos_g, sin_g)
            x1, xp1, logits_t = _proj_ln([o], [gqa_w_out[i].astype(BF16)], xf,
                                         g_mix, b_mix, wr_hi, wr_lo, tm=512)
        z = _moe(x1, xp1, logits_t, moe_w1, moe_w3, moe_w2, layer, batch, seq)
        xf, xb = _ln(z, ln_ffn_g[layer][None, :], ln_ffn_b[layer][None, :], tm=512)
    return xf.reshape(batch, seq, d)
```

```python
import functools

import jax
import jax.numpy as jnp
from jax import lax
from jax.experimental import pallas as pl
from jax.experimental.pallas import tpu as pltpu

F32 = jnp.float32
BF16 = jnp.bfloat16

GRID_W = 64
ROPE_THETA = 10000.0
LN_EPS = 1e-5
RMS_EPS = 1e-6
DEPTH = 4
DEEPNORM_ALPHA = (2.0 * DEPTH) ** 0.25

MLA_HEADS = 8
MLA_Q_RANK = 512
MLA_KV_RANK = 256
MLA_NOPE_DIM = 128
MLA_ROPE_DIM = 64
MLA_V_DIM = 128
MLA_QK_PAD = 256

GLA_HEADS = 4
GLA_DK = 128
GLA_DV = 256
GLA_GATE_RANK = 16
GLA_GATE_TAU = 16.0
GLA_CHUNK = 64
GLA_GROUP = 256
GLA_SCAN_UNROLL = 8
GLA_GROUP_UNROLL = 4

GQA_HEADS = 16
GQA_KV_HEADS = 4
GQA_HEAD_DIM = 128

N_EXPERTS = 16
CAPACITY_FACTOR = 2

LANES = 128
VMEM_LIMIT_BYTES = 56 * 1024 * 1024

LOG2_E = 1.4426950408889634
ATTN_SAFE_LOG2_RANGE = 60.0

H_CQ = 0
H_GQ = 512
H_GK = 1024
H_CKV = 1536
H_TAIL = 1792
H_GV = 2048
H_GR = 3072
H_WIDTH = 4096


def _cparams(sem):
    return pltpu.CompilerParams(dimension_semantics=sem, vmem_limit_bytes=VMEM_LIMIT_BYTES)


def _nt_dot(a, b):
    return lax.dot_general(a, b, (((1,), (1,)), ((), ())), preferred_element_type=F32)


def _tn_dot(a, b):
    return lax.dot_general(a, b, (((0,), (0,)), ((), ())), preferred_element_type=F32)


def _rope(x, cos, sin_signed):
    lane = lax.broadcasted_iota(jnp.int32, x.shape, 1)
    partner = jnp.where((lane & 1) == 0, pltpu.roll(x, LANES - 1, 1), pltpu.roll(x, 1, 1))
    return x * cos + partner * sin_signed


def _rms(x, gain):
    return x * lax.rsqrt(jnp.mean(x * x, axis=-1, keepdims=True) + RMS_EPS) * gain


def _store_token_major(ref, val, first_row=0):
    count, n = val.shape[0], val.shape[1] // LANES
    for j in range(n):
        ref[pl.ds(first_row * n + j, count, stride=n), :] = val[:, j * LANES:(j + 1) * LANES]


def _layer_norm(z, g, b):
    mu = jnp.mean(z, axis=-1, keepdims=True)
    zc = z - mu
    var = jnp.mean(zc * zc, axis=-1, keepdims=True)
    return zc * lax.rsqrt(var + LN_EPS) * g + b


def _mm_kernel(x_ref, w_ref, o_ref):
    o_ref[...] = jnp.dot(x_ref[...], w_ref[...], preferred_element_type=F32).astype(o_ref.dtype)


def _matmul(x, w, out_dtype, tm, tn):
    m, k = x.shape
    n = w.shape[1]
    tm, tn = min(tm, m), min(tn, n)
    return pl.pallas_call(
        _mm_kernel,
        grid=(n // tn, m // tm),
        in_specs=[pl.BlockSpec((tm, k), lambda j, i: (i, 0)),
                  pl.BlockSpec((k, tn), lambda j, i: (0, j))],
        out_specs=pl.BlockSpec((tm, tn), lambda j, i: (i, j)),
        out_shape=jax.ShapeDtypeStruct((m, n), out_dtype),
        compiler_params=_cparams(("parallel", "parallel")),
        name="dense_matmul",
    )(x, w)


def _qkv_kernel(x_ref, w_ref, g_ref, ones_ref, cos_ref, sin_ref, o_ref, *, heads_per_tile):
    acc = jnp.dot(x_ref[...], w_ref[...], preferred_element_type=F32)
    sq = acc * acc
    sq_hi = sq.astype(BF16)
    sq_lo = (sq - sq_hi.astype(F32)).astype(BF16)
    ones_bd = ones_ref[...]
    ms = (jnp.dot(sq_hi, ones_bd, preferred_element_type=F32)
          + jnp.dot(sq_lo, ones_bd, preferred_element_type=F32)) * (1.0 / GQA_HEAD_DIM)
    y = acc * lax.rsqrt(ms + RMS_EPS) * g_ref[0:1, :]
    cos = cos_ref[...]
    sin = sin_ref[...]
    for hd in range(heads_per_tile):
        sl = slice(hd * GQA_HEAD_DIM, (hd + 1) * GQA_HEAD_DIM)
        yh = y[:, sl]
        rot = yh * cos + pltpu.roll(yh, GQA_HEAD_DIM // 2, 1) * sin
        o_ref[:, sl] = jnp.where(g_ref[1:2, sl] > 0.5, rot, acc[:, sl]).astype(o_ref.dtype)


def _qkv_proj(xb, w, gains, cos, sin, seq, tm):
    m, k = xb.shape
    n = w.shape[1]
    tn = 4 * GQA_HEAD_DIM
    tm = min(tm, seq)
    n_tiles = n // tn
    pos_blocks = seq // tm
    kern = functools.partial(_qkv_kernel, heads_per_tile=4)
    col_head = jnp.arange(tn, dtype=jnp.int32) // GQA_HEAD_DIM
    ones_bd = (col_head[:, None] == col_head[None, :]).astype(BF16)
    return pl.pallas_call(
        kern,
        grid=(n_tiles, m // tm),
        in_specs=[pl.BlockSpec((tm, k), lambda j, i: (i, 0)),
                  pl.BlockSpec((k, tn), lambda j, i: (0, j)),
                  pl.BlockSpec((None, 2, tn), lambda j, i: (j, 0, 0)),
                  pl.BlockSpec((tn, tn), lambda j, i: (0, 0)),
                  pl.BlockSpec((tm, LANES), lambda j, i: (i % pos_blocks, 0)),
                  pl.BlockSpec((tm, LANES), lambda j, i: (i % pos_blocks, 0))],
        out_specs=pl.BlockSpec((tm, tn), lambda j, i: (i, j)),
        out_shape=jax.ShapeDtypeStruct((m, n), BF16),
        compiler_params=_cparams(("parallel", "parallel")),
        name="gqa_qkv_proj",
    )(xb, w, gains, ones_bd, cos, sin)


def _attn_kernel(q_ref, k_ref, v_ref, o_ref, vx_s, kmax_s, *, group, dqk, dv, kc_single, kc_online):
    tq = q_ref.shape[0]
    seq = k_ref.shape[0]
    rows = group * tq

    @pl.when(pl.program_id(2) == 0)
    def _():
        lane = lax.broadcasted_iota(jnp.int32, (seq, LANES), 1)
        vx_s[:, :dv] = v_ref[...]
        vx_s[:, dv:] = jnp.where(lane == 0, 1.0, 0.0).astype(BF16)
        k32 = k_ref[...].astype(F32)
        k_sq = jnp.sum(k32 * k32, axis=1, keepdims=True)
        kmax_s[...] = jnp.max(k_sq, axis=0, keepdims=True)

    q = jnp.concatenate([q_ref[:, g * dqk:(g + 1) * dqk] for g in range(group)], axis=0)
    q32 = q.astype(F32)
    bound = jnp.sqrt(jnp.sum(q32 * q32, axis=1, keepdims=True) * kmax_s[...])
    single_pass = jnp.max(bound) <= ATTN_SAFE_LOG2_RANGE

    def write(acc):
        o = acc[:, :dv] / acc[:, dv:dv + 1]
        for g in range(group):
            o_ref[:, g * dv:(g + 1) * dv] = o[g * tq:(g + 1) * tq, :].astype(o_ref.dtype)

    @pl.when(single_pass)
    def _():
        kc = kc_single
        acc = jnp.zeros((rows, dv + LANES), F32)
        for c in range(seq // kc):
            p = jnp.exp2(_nt_dot(q, k_ref[c * kc:(c + 1) * kc, :]) - bound)
            acc = acc + jnp.dot(p.astype(BF16), vx_s[c * kc:(c + 1) * kc, :], preferred_element_type=F32)
        write(acc)

    @pl.when(jnp.logical_not(single_pass))
    def _():
        kc = kc_online
        m = jnp.full((rows, 1), -jnp.inf, F32)
        acc = jnp.zeros((rows, dv + LANES), F32)
        for c in range(seq // kc):
            s = _nt_dot(q, k_ref[c * kc:(c + 1) * kc, :])
            m_new = jnp.maximum(m, jnp.max(s, axis=1, keepdims=True))
            p = jnp.exp2(s - m_new)
            acc = acc * jnp.exp2(m - m_new) + jnp.dot(p.astype(BF16), vx_s[c * kc:(c + 1) * kc, :],
                                                      preferred_element_type=F32)
            m = m_new
        write(acc)


def _attention(q_arr, k_arr, v_arr, *, batch, seq, kv_heads, group, dqk, dv,
               q_off, k_off, v_off, tq):
    tq = min(tq, seq)
    nq = seq // tq
    kern = functools.partial(_attn_kernel, group=group, dqk=dqk, dv=dv,
                             kc_single=min(256, seq), kc_online=min(1024, seq))
    return pl.pallas_call(
        kern,
        grid=(batch, kv_heads, nq),
        in_specs=[pl.BlockSpec((tq, group * dqk), lambda b, h, i: (b * nq + i, q_off + h)),
                  pl.BlockSpec((seq, dqk), lambda b, h, i: (b, k_off + h)),
                  pl.BlockSpec((seq, dv), lambda b, h, i: (b, v_off + h))],
        out_specs=pl.BlockSpec((tq, group * dv), lambda b, h, i: (b * nq + i, h)),
        out_shape=jax.ShapeDtypeStruct((batch * seq, kv_heads * group * dv), BF16),
        scratch_shapes=[pltpu.VMEM((seq, dv + LANES), BF16), pltpu.VMEM((1, 1), F32)],
        compiler_params=_cparams(("arbitrary", "arbitrary", "arbitrary")),
        name="softmax_attention",
    )(q_arr, k_arr, v_arr)


def _mla_prep_kernel(cq_ref, ckv_ref, tail_ref, qn_ref, wuq_ref, kvn_ref, wukv_ref, wg_ref, gb_ref,
                     cos_ref, sin_ref, q_ref, k_ref, v_ref, la_ref):
    cos = cos_ref[...]
    sin = sin_ref[...]
    scale = (MLA_NOPE_DIM + MLA_ROPE_DIM) ** -0.5 * LOG2_E
    q = jnp.dot(_rms(cq_ref[...], qn_ref[...]).astype(BF16), wuq_ref[...],
                preferred_element_type=F32) * scale
    kv = jnp.dot(_rms(ckv_ref[...], kvn_ref[...]).astype(BF16), wukv_ref[...],
                 preferred_element_type=F32)
    tail = tail_ref[...]
    lane = lax.broadcasted_iota(jnp.int32, tail.shape, 1)
    k_pe = jnp.where(lane < MLA_ROPE_DIM, _rope(tail, cos, sin), 0.0).astype(BF16)
    for hd in range(MLA_HEADS):
        c0 = hd * MLA_QK_PAD
        q_ref[:, c0:c0 + LANES] = q[:, c0:c0 + LANES].astype(BF16)
        q_ref[:, c0 + LANES:c0 + 2 * LANES] = _rope(q[:, c0 + LANES:c0 + 2 * LANES], cos, sin).astype(BF16)
        k_ref[:, c0:c0 + LANES] = kv[:, hd * LANES:(hd + 1) * LANES].astype(BF16)
        k_ref[:, c0 + LANES:c0 + 2 * LANES] = k_pe
    nv = MLA_HEADS * MLA_V_DIM
    v_ref[...] = kv[:, nv:].astype(BF16)
    gate = jnp.dot(tail.astype(BF16), wg_ref[...], preferred_element_type=F32) + gb_ref[...]
    log_sig = jnp.minimum(gate, 0.0) - jnp.log1p(jnp.exp(-jnp.abs(gate)))
    la_ref[...] = log_sig / GLA_GATE_TAU


def _mla_prep(h, qn, wuq, kvn, wukv, wg, gb, cos, sin, seq, tm):
    m = h.shape[0]
    tm = min(tm, seq)
    pos_blocks = seq // tm
    qw = MLA_HEADS * MLA_QK_PAD
    vw = MLA_HEADS * MLA_V_DIM
    gw = 2 * GLA_HEADS * GLA_DK
    const = lambda i: (0, 0)
    return pl.pallas_call(
        _mla_prep_kernel,
        grid=(m // tm,),
        in_specs=[pl.BlockSpec((tm, MLA_Q_RANK), lambda i: (i, H_CQ // MLA_Q_RANK)),
                  pl.BlockSpec((tm, MLA_KV_RANK), lambda i: (i, H_CKV // MLA_KV_RANK)),
                  pl.BlockSpec((tm, LANES), lambda i: (i, H_TAIL // LANES)),
                  pl.BlockSpec((1, MLA_Q_RANK), const),
                  pl.BlockSpec((MLA_Q_RANK, qw), const),
                  pl.BlockSpec((1, MLA_KV_RANK), const),
                  pl.BlockSpec((MLA_KV_RANK, 2 * vw), const),
                  pl.BlockSpec((LANES, gw), const),
                  pl.BlockSpec((1, gw), const),
                  pl.BlockSpec((tm, LANES), lambda i: (i % pos_blocks, 0)),
                  pl.BlockSpec((tm, LANES), lambda i: (i % pos_blocks, 0))],
        out_specs=[pl.BlockSpec((tm, qw), lambda i: (i, 0)),
                   pl.BlockSpec((tm, qw), lambda i: (i, 0)),
                   pl.BlockSpec((tm, vw), lambda i: (i, 0)),
                   pl.BlockSpec((tm, gw), lambda i: (i, 0))],
        out_shape=[jax.ShapeDtypeStruct((m, qw), BF16),
                   jax.ShapeDtypeStruct((m, qw), BF16),
                   jax.ShapeDtypeStruct((m, vw), BF16),
                   jax.ShapeDtypeStruct((m, gw), F32)],
        compiler_params=_cparams(("parallel",)),
        name="mla_prep",
    )(h, h, h, qn, wuq, kvn, wukv, wg, gb, cos, sin)


def _split3(x):
    hi = x.astype(BF16)
    r1 = x - hi.astype(F32)
    mid = r1.astype(BF16)
    lo = (r1 - mid.astype(F32)).astype(BF16)
    return hi, mid, lo


def _gla_kernel(q_ref, k_ref, v_ref, gr_ref, laf_ref, lab_ref, gn_ref, o_ref,
                of_s, ob_s, qf_s, qb_s, kf_s, kb_s, df_s, db_s, stf_s, stb_s):
    seq = q_ref.shape[0]
    c = GLA_CHUNK
    r = min(GLA_GROUP, seq)
    cpg = r // c
    n_groups = seq // r
    group_unroll = min(GLA_GROUP_UNROLL, n_groups)
    scan_unroll = min(GLA_SCAN_UNROLL, seq // c)
    n_chunks = seq // c
    scale = GLA_DK ** -0.5

    row = lax.broadcasted_iota(jnp.int32, (r, r), 0)
    col = lax.broadcasted_iota(jnp.int32, (r, r), 1)
    same = (row // c) == (col // c)
    tril = same & (col <= row)
    triu = same & (col >= row)
    tril_b = tril.astype(F32).astype(BF16)
    triu_b = triu.astype(F32).astype(BF16)

    def group_pair(gp, carry):
        for u in range(group_unroll):
            group_body(gp * group_unroll + u)
        return carry

    def group_body(gi):
        r0 = pl.multiple_of(gi * r, r)
        rows = pl.ds(r0, r)
        q = q_ref[rows, :] * scale
        k = k_ref[rows, :]
        v = v_ref[rows, :].astype(BF16)
        for la_ref, mask, mask_b, edge, o_s, q_s, k_s, d_s in (
                (laf_ref, tril, tril_b, c - 1, of_s, qf_s, kf_s, df_s),
                (lab_ref, triu, triu_b, 0, ob_s, qb_s, kb_s, db_s)):
            hi, mid, lo = _split3(la_ref[rows, :])
            b = (jnp.dot(mask_b, hi, preferred_element_type=F32)
                 + jnp.dot(mask_b, mid, preferred_element_type=F32)
                 + jnp.dot(mask_b, lo, preferred_element_type=F32))
            b_edge = jnp.concatenate(
                [jnp.broadcast_to(b[ci * c + edge:ci * c + edge + 1, :], (c, GLA_DK)) for ci in range(cpg)],
                axis=0)
            q_in = (q * jnp.exp(b)).astype(BF16)
            k_in = (k * jnp.exp(-b)).astype(BF16)
            k_st = (k * jnp.exp(b_edge - b)).astype(BF16)
            att = jnp.where(mask, _nt_dot(q_in, k_in), 0.0)
            o_s[rows, :] = jnp.dot(att.astype(BF16), v, preferred_element_type=F32)
            q_s[rows, :] = q_in
            k_s[rows, :] = k_st
            for ci in range(cpg):
                d_s[pl.ds(gi * cpg + ci, 1), :] = jnp.exp(b[ci * c + edge:ci * c + edge + 1, :])

    lax.fori_loop(0, n_groups // group_unroll, group_pair, 0)

    stf_s[...] = jnp.zeros_like(stf_s)
    stb_s[...] = jnp.zeros_like(stb_s)

    def chunk_body(i, carry):
        for u in range(scan_unroll):
            fwd_n = i * scan_unroll + u
            for n, o_s, q_s, k_s, d_s, st_s in ((fwd_n, of_s, qf_s, kf_s, df_s, stf_s),
                                                (n_chunks - 1 - fwd_n, ob_s, qb_s, kb_s, db_s, stb_s)):
                rows = pl.ds(pl.multiple_of(n * c, c), c)
                st = st_s[...]
                o_s[rows, :] += _nt_dot(q_s[rows, :], st.astype(BF16))
                ds = _tn_dot(v_ref[rows, :].astype(BF16), k_s[rows, :])
                st_s[...] = d_s[pl.ds(n, 1), :] * st + ds
        return carry

    lax.fori_loop(0, n_chunks // scan_unroll, chunk_body, 0)

    gain = gn_ref[...]

    def out_body(gi, carry):
        rows = pl.ds(pl.multiple_of(gi * r, r), r)
        o = _rms(of_s[rows, :] + ob_s[rows, :], gain)
        gr = gr_ref[rows, :]
        o_ref[rows, :] = (o * (gr * jax.nn.sigmoid(gr))).astype(o_ref.dtype)
        return carry

    lax.fori_loop(0, n_groups, out_body, 0)


def _gla(h, la, gn, batch, seq):
    dk, dv, nh = GLA_DK, GLA_DV, GLA_HEADS
    n_chunks = seq // GLA_CHUNK
    return pl.pallas_call(
        _gla_kernel,
        grid=(batch, nh),
        in_specs=[pl.BlockSpec((seq, dk), lambda b, hd: (b, H_GQ // dk + hd)),
                  pl.BlockSpec((seq, dk), lambda b, hd: (b, H_GK // dk + hd)),
                  pl.BlockSpec((seq, dv), lambda b, hd: (b, H_GV // dv + hd)),
                  pl.BlockSpec((seq, dv), lambda b, hd: (b, H_GR // dv + hd)),
                  pl.BlockSpec((seq, dk), lambda b, hd: (b, hd)),
                  pl.BlockSpec((seq, dk), lambda b, hd: (b, nh + hd)),
                  pl.BlockSpec((None, 1, dv), lambda b, hd: (hd, 0, 0))],
        out_specs=pl.BlockSpec((seq, dv), lambda b, hd: (b, hd)),
        out_shape=jax.ShapeDtypeStruct((batch * seq, nh * dv), BF16),
        scratch_shapes=[pltpu.VMEM((seq, dv), F32), pltpu.VMEM((seq, dv), F32),
                        pltpu.VMEM((seq, dk), BF16), pltpu.VMEM((seq, dk), BF16),
                        pltpu.VMEM((seq, dk), BF16), pltpu.VMEM((seq, dk), BF16),
                        pltpu.VMEM((n_chunks, dk), F32), pltpu.VMEM((n_chunks, dk), F32),
                        pltpu.VMEM((dv, dk), F32), pltpu.VMEM((dv, dk), F32)],
        compiler_params=_cparams(("arbitrary", "arbitrary")),
        name="gla_bidirectional",
    )(h, h, h, h, la, la, gn)


def _proj_ln_kernel(*refs, n_in):
    a_refs = refs[:n_in]
    w_refs = refs[n_in:2 * n_in]
    x_ref, g_ref, b_ref, wrh_ref, wrl_ref, xo_ref, xp_ref, lg_ref = refs[2 * n_in:]
    mix = jnp.dot(a_refs[0][...], w_refs[0][...], preferred_element_type=F32)
    for a_ref, w_ref in zip(a_refs[1:], w_refs[1:]):
        mix = mix + jnp.dot(a_ref[...], w_ref[...], preferred_element_type=F32)
    y = _layer_norm(DEEPNORM_ALPHA * x_ref[...] + mix, g_ref[...], b_ref[...])
    xo_ref[...] = y
    y_hi = y.astype(BF16)
    y_hi32 = y_hi.astype(F32)
    half = y.shape[1] // 2
    bits = lax.bitcast_convert_type(y_hi32, jnp.int32)
    _store_token_major(xp_ref, lax.shift_right_logical(bits[:, :half], 16) | bits[:, half:])
    y_lo = (y - y_hi32).astype(BF16)
    wrh = wrh_ref[...]
    lg_ref[...] = _nt_dot(wrh, y_hi) + _nt_dot(wrh, y_lo) + _nt_dot(wrl_ref[...], y_hi)


def _proj_ln(acts, weights, x, g, b, wr_hi, wr_lo, tm):
    m, d = x.shape
    tm = min(tm, m)
    n_in = len(acts)
    n_e = wr_hi.shape[0]
    np_ = d // (2 * LANES)
    const = lambda i: (0, 0)
    in_specs = ([pl.BlockSpec((tm, a.shape[1]), lambda i: (i, 0)) for a in acts]
                + [pl.BlockSpec(w.shape, const) for w in weights]
                + [pl.BlockSpec((tm, d), lambda i: (i, 0)),
                   pl.BlockSpec((1, d), const), pl.BlockSpec((1, d), const),
                   pl.BlockSpec((n_e, d), const), pl.BlockSpec((n_e, d), const)])
    return pl.pallas_call(
        functools.partial(_proj_ln_kernel, n_in=n_in),
        grid=(m // tm,),
        in_specs=in_specs,
        out_specs=[pl.BlockSpec((tm, d), lambda i: (i, 0)),
                   pl.BlockSpec((tm * np_, LANES), lambda i: (i, 0)),
                   pl.BlockSpec((n_e, tm), lambda i: (0, i))],
        out_shape=[jax.ShapeDtypeStruct((m, d), F32),
                   jax.ShapeDtypeStruct((m * np_, LANES), jnp.int32),
                   jax.ShapeDtypeStruct((n_e, m), F32)],
        compiler_params=_cparams(("parallel",)),
        name="out_proj_layernorm_router",
    )(*acts, *weights, x, g, b, wr_hi, wr_lo)


def _ln_kernel(z_ref, g_ref, b_ref, xo_ref, xb_ref):
    y = _layer_norm(z_ref[...], g_ref[...], b_ref[...])
    xo_ref[...] = y
    xb_ref[...] = y.astype(BF16)


def _ln(z, g, b, tm):
    m, d = z.shape
    tm = min(tm, m)
    const = lambda i: (0, 0)
    return pl.pallas_call(
        _ln_kernel,
        grid=(m // tm,),
        in_specs=[pl.BlockSpec((tm, d), lambda i: (i, 0)),
                  pl.BlockSpec((1, d), const), pl.BlockSpec((1, d), const)],
        out_specs=[pl.BlockSpec((tm, d), lambda i: (i, 0)),
                   pl.BlockSpec((tm, d), lambda i: (i, 0))],
        out_shape=[jax.ShapeDtypeStruct((m, d), F32), jax.ShapeDtypeStruct((m, d), BF16)],
        compiler_params=_cparams(("parallel",)),
        name="layernorm",
    )(z, g, b)


def _route_kernel(lg_ref, idx_ref, gate_ref, aff_s, cum_s, start_s, *, cap):
    lg = lg_ref[...]
    ex = jnp.exp(lg - jnp.max(lg, axis=0, keepdims=True))
    aff = ex / jnp.sum(ex, axis=0, keepdims=True)
    bits = lax.bitcast_convert_type(aff, jnp.int32)
    n_e, seq = lg.shape
    nblk = seq // LANES
    capf = float(cap)

    def bisect(_, carry):
        lo, hi = carry
        mid = lo + ((hi - lo + 1) >> 1)
        cnt = jnp.sum(jnp.where(bits >= mid, 1.0, 0.0), axis=1, keepdims=True)
        ok = cnt >= capf
        return jnp.where(ok, mid, lo), jnp.where(ok, hi, mid - 1)

    lo0 = jnp.zeros((n_e, 1), jnp.int32)
    hi0 = jnp.full((n_e, 1), 0x7F800000, jnp.int32)
    thr, _ = lax.fori_loop(0, 32, bisect, (lo0, hi0))

    gt = bits > thr
    eq = bits == thr
    need = capf - jnp.sum(jnp.where(gt, 1.0, 0.0), axis=1, keepdims=True)

    r_i = lax.broadcasted_iota(jnp.int32, (LANES, LANES), 0)
    c_i = lax.broadcasted_iota(jnp.int32, (LANES, LANES), 1)
    upper = (r_i <= c_i).astype(F32).astype(BF16)
    eq_off = jnp.zeros((n_e, 1), F32)
    sel_off = jnp.zeros((n_e, 1), F32)
    for j in range(nblk):
        blk = slice(j * LANES, (j + 1) * LANES)
        eq01 = jnp.where(eq[:, blk], 1.0, 0.0)
        eq_cum = jnp.dot(eq01.astype(BF16), upper, preferred_element_type=F32) + eq_off
        eq_off = eq_cum[:, LANES - 1:LANES]
        tie_rank = eq_cum - eq01
        sel01 = jnp.where(gt[:, blk] | (eq[:, blk] & (tie_rank < need)), 1.0, 0.0)
        sel_cum = jnp.dot(sel01.astype(BF16), upper, preferred_element_type=F32) + sel_off
        rows = slice(j * n_e, (j + 1) * n_e)
        cum_s[rows, :] = sel_cum
        start_s[rows, :] = jnp.broadcast_to(sel_off, (n_e, LANES))
        aff_s[rows, :] = aff[:, blk]
        sel_off = sel_cum[:, LANES - 1:LANES]

    slot_row = lax.broadcasted_iota(jnp.int32, (1, cap), 1).astype(F32)
    slot_col = lax.broadcasted_iota(jnp.int32, (cap, 1), 0).astype(F32)
    lane = lax.broadcasted_iota(jnp.int32, (cap, LANES), 1).astype(F32)
    blk_id = lax.broadcasted_iota(jnp.int32, (nblk, LANES), 0).astype(F32).astype(BF16)

    def extract(e, carry):
        rows = pl.ds(e, nblk, stride=n_e)
        cum_e = cum_s[rows, :]
        start_e = start_s[rows, :][:, 0:1]
        end_e = cum_e[:, LANES - 1:LANES]
        in_blk = jnp.where((start_e <= slot_row) & (slot_row < end_e), 1.0, 0.0).astype(BF16)
        cum_hi = jnp.floor(cum_e * (1.0 / 32.0))
        cum_lo = cum_e - 32.0 * cum_hi
        g = 32.0 * _tn_dot(in_blk, cum_hi.astype(BF16)) + _tn_dot(in_blk, cum_lo.astype(BF16))
        blk_of_slot = _tn_dot(in_blk, blk_id)[:, 0:1]
        pos = jnp.sum(jnp.where(g <= slot_col, 1.0, 0.0), axis=1, keepdims=True)
        a_hi, a_mid, a_lo = _split3(aff_s[rows, :])
        a = (_tn_dot(in_blk, a_hi) + _tn_dot(in_blk, a_mid)) + _tn_dot(in_blk, a_lo)
        gate = jnp.sum(jnp.where(lane == pos, a, 0.0), axis=1, keepdims=True)
        out_rows = pl.ds(pl.multiple_of(e * cap, cap), cap)
        idx_ref[out_rows, :] = (float(LANES) * blk_of_slot + pos).astype(jnp.int32)
        gate_ref[out_rows, :] = gate
        return carry

    lax.fori_loop(0, n_e, extract, 0)


def _route(logits_t, batch, seq, cap):
    n_e = logits_t.shape[0]
    kern = functools.partial(_route_kernel, cap=cap)
    scratch_rows = (seq // LANES) * n_e
    return pl.pallas_call(
        kern,
        grid=(batch,),
        in_specs=[pl.BlockSpec((n_e, seq), lambda b: (0, b))],
        out_specs=[pl.BlockSpec((n_e * cap, 1), lambda b: (b, 0)),
                   pl.BlockSpec((n_e * cap, 1), lambda b: (b, 0))],
        out_shape=[jax.ShapeDtypeStruct((batch * n_e * cap, 1), jnp.int32),
                   jax.ShapeDtypeStruct((batch * n_e * cap, 1), F32)],
        scratch_shapes=[pltpu.VMEM((scratch_rows, LANES), F32), pltpu.VMEM((scratch_rows, LANES), F32),
                        pltpu.VMEM((scratch_rows, LANES), F32)],
        compiler_params=_cparams(("parallel",)),
        name="expert_choice_route",
    )(logits_t)


GATHER_UNROLL = 32


def _ffn_kernel(idx_cur, idx_nxt, xp_hbm, w1_ref, w3_ref, w2_ref, gate_ref, o_ref,
                land, x_lo, x_hi, sem, *, cap, seq, bpt, issue_steps, unpack_rows):
    m_tiles = pl.num_programs(1)
    f = pl.program_id(2)
    t = pl.program_id(0) * m_tiles + pl.program_id(1)
    n_tiles = pl.num_programs(0) * m_tiles
    slot = t % 2
    tm = bpt * cap
    half = x_lo.shape[1]
    npk = half // LANES

    def row_copy(dst_slot, r, src_row):
        return pltpu.make_async_copy(xp_hbm.at[pl.ds(pl.multiple_of(src_row * npk, npk), npk), :],
                                     land.at[dst_slot, pl.ds(pl.multiple_of(r * npk, npk), npk), :],
                                     sem.at[dst_slot])

    def issue(idx_ref, tile, dst_slot, bb, s0, n):
        src_base = ((tile % m_tiles) * bpt + bb) * seq
        dst_base = bb * cap

        def body(i, carry):
            for u in range(GATHER_UNROLL):
                s = s0 + i * GATHER_UNROLL + u
                row_copy(dst_slot, dst_base + s, src_base + idx_ref[bb, 0, s]).start()
            return carry

        lax.fori_loop(0, n // GATHER_UNROLL, body, 0)

    @pl.when((t == 0) & (f == 0))
    def _():
        for bb in range(bpt):
            issue(idx_cur, t, slot, bb, 0, cap)

    @pl.when(f == 0)
    def _():
        pltpu.make_async_copy(xp_hbm.at[pl.ds(0, tm * npk), :], land.at[slot], sem.at[slot]).wait()

        def unpack(i, carry):
            r0 = pl.multiple_of(i * unpack_rows, unpack_rows)
            rows = pl.ds(r0, unpack_rows)
            for j in range(npk):
                w = land[slot, pl.ds(r0 * npk + j, unpack_rows, stride=npk), :]
                cols = slice(j * LANES, (j + 1) * LANES)
                x_lo[rows, cols] = lax.bitcast_convert_type(w << 16, F32).astype(BF16)
                x_hi[rows, cols] = lax.bitcast_convert_type(w & jnp.int32(-65536), F32).astype(BF16)
            return carry

        lax.fori_loop(0, tm // unpack_rows, unpack, 0)
        o_ref[...] = jnp.zeros_like(o_ref)

    per_batch = issue_steps // bpt
    n = cap // per_batch
    for bb in range(bpt):
        @pl.when((f >= bb * per_batch) & (f < (bb + 1) * per_batch) & (t + 1 < n_tiles))
        def _(bb=bb):
            issue(idx_nxt, t + 1, 1 - slot, bb, (f - bb * per_batch) * n, n)

    w1 = w1_ref[...].astype(BF16)
    w3 = w3_ref[...].astype(BF16)
    lo, hi = x_lo[...], x_hi[...]
    h1 = (jnp.dot(lo, w1[:half], preferred_element_type=F32)
          + jnp.dot(hi, w1[half:], preferred_element_type=F32))
    h3 = (jnp.dot(lo, w3[:half], preferred_element_type=F32)
          + jnp.dot(hi, w3[half:], preferred_element_type=F32))
    hdn = (h1 * jax.nn.sigmoid(h1)) * h3
    part = jnp.dot(hdn.astype(BF16), w2_ref[...].astype(BF16), preferred_element_type=F32)
    is_last = f == pl.num_programs(2) - 1
    gate = jnp.where(is_last, gate_ref[...], 1.0)
    o_ref[...] = (o_ref[...] + part) * gate


def _ffn(idx_eb, xp, w1, w3, w2, layer, gate_col, batch, seq, cap, tf):
    _, n_e, d, ff = w1.shape
    bpt = min(2, batch)
    tm = bpt * cap
    mt = batch // bpt
    tf = min(tf, ff // bpt)
    nf = ff // tf
    assert nf >= 2 and ff % tf == 0
    issue_steps = bpt * min(2, nf // bpt)
    assert cap % (issue_steps // bpt * GATHER_UNROLL) == 0
    n_tiles = n_e * mt
    kern = functools.partial(_ffn_kernel, cap=cap, seq=seq, bpt=bpt, issue_steps=issue_steps,
                             unpack_rows=min(256, tm))
    smem_idx = lambda fn: pl.BlockSpec((bpt, 1, cap), fn, memory_space=pltpu.SMEM)
    return pl.pallas_call(
        kern,
        grid=(n_e, mt, nf),
        in_specs=[smem_idx(lambda e, m, f: (e * mt + m, 0, 0)),
                  smem_idx(lambda e, m, f: (jnp.minimum(e * mt + m + 1, n_tiles - 1), 0, 0)),
                  pl.BlockSpec(memory_space=pl.ANY),
                  pl.BlockSpec((None, None, d, tf), lambda e, m, f: (layer, e, 0, f)),
                  pl.BlockSpec((None, None, d, tf), lambda e, m, f: (layer, e, 0, f)),
                  pl.BlockSpec((None, None, tf, d), lambda e, m, f: (layer, e, f, 0)),
                  pl.BlockSpec((tm, 1), lambda e, m, f: (e * mt + m, 0))],
        out_specs=pl.BlockSpec((tm, d), lambda e, m, f: (e * mt + m, 0)),
        out_shape=jax.ShapeDtypeStruct((n_e * batch * cap, d), F32),
        scratch_shapes=[pltpu.VMEM((2, tm * (d // (2 * LANES)), LANES), jnp.int32),
                        pltpu.VMEM((tm, d // 2), BF16), pltpu.VMEM((tm, d // 2), BF16),
                        pltpu.SemaphoreType.DMA((2,))],
        compiler_params=_cparams(("arbitrary", "arbitrary", "arbitrary")),
        name="moe_expert_ffn",
    )(idx_eb, idx_eb, xp, w1, w3, w2, gate_col)


COMBINE_UNROLL = 8


def _combine_kernel(idx_ref, y_ref, x_hbm, z_hbm, acc, sem, *, cap, seq, n_e, row_block):
    b = pl.program_id(0)
    e = pl.program_id(1)

    @pl.when(e == 0)
    def _():
        cp = pltpu.make_async_copy(x_hbm.at[pl.ds(b * seq, seq), :], acc, sem)
        cp.start()
        cp.wait()

        def scale(i, carry):
            rows = pl.ds(pl.multiple_of(i * row_block, row_block), row_block)
            acc[rows, :] = acc[rows, :] * DEEPNORM_ALPHA
            return carry

        lax.fori_loop(0, seq // row_block, scale, 0)

    def add_rows(i, carry):
        s0 = i * COMBINE_UNROLL
        toks = [idx_ref[0, 0, s0 + u] for u in range(COMBINE_UNROLL)]
        sums = [acc[pl.ds(toks[u], 1), :] + y_ref[pl.ds(s0 + u, 1), :] for u in range(COMBINE_UNROLL)]
        for u in range(COMBINE_UNROLL):
            acc[pl.ds(toks[u], 1), :] = sums[u]
        return carry

    lax.fori_loop(0, cap // COMBINE_UNROLL, add_rows, 0)

    @pl.when(e == n_e - 1)
    def _():
        cp = pltpu.make_async_copy(acc, z_hbm.at[pl.ds(b * seq, seq), :], sem)
        cp.start()
        cp.wait()


def _combine(idx_be, y, x, batch, seq, cap, n_e):
    d = x.shape[1]
    kern = functools.partial(_combine_kernel, cap=cap, seq=seq, n_e=n_e, row_block=min(256, seq))
    return pl.pallas_call(
        kern,
        grid=(batch, n_e),
        in_specs=[pl.BlockSpec((1, 1, cap), lambda b, e: (b * n_e + e, 0, 0), memory_space=pltpu.SMEM),
                  pl.BlockSpec((cap, d), lambda b, e: (e * batch + b, 0)),
                  pl.BlockSpec(memory_space=pl.ANY)],
        out_specs=pl.BlockSpec(memory_space=pl.ANY),
        out_shape=jax.ShapeDtypeStruct((batch * seq, d), F32),
        scratch_shapes=[pltpu.VMEM((seq, d), F32), pltpu.SemaphoreType.DMA(())],
        compiler_params=_cparams(("arbitrary", "arbitrary")),
        name="moe_combine",
    )(idx_be, y, x)


def _rope_tables(seq, rot_dim, split_halves=False):
    rows = seq // GRID_W
    row = jnp.repeat(jnp.arange(rows, dtype=jnp.int32), GRID_W).astype(F32)
    col = jnp.tile(jnp.arange(GRID_W, dtype=jnp.int32), rows).astype(F32)
    half = rot_dim // 2
    inv_freq = ROPE_THETA ** (-jnp.arange(0, half, 2, dtype=F32) / half)
    ang = jnp.concatenate([row[:, None] * inv_freq, col[:, None] * inv_freq], axis=-1)
    cos, sin = jnp.cos(ang), jnp.sin(ang)
    if split_halves:
        cos2 = jnp.concatenate([cos, cos], axis=1)
        sin2 = jnp.concatenate([-sin, sin], axis=1)
    else:
        cos2 = jnp.repeat(cos, 2, axis=1)
        sin2 = jnp.stack([-sin, sin], axis=-1).reshape(seq, rot_dim)
    pad = LANES - rot_dim
    if pad:
        cos2 = jnp.concatenate([cos2, jnp.ones((seq, pad), F32)], axis=1)
        sin2 = jnp.concatenate([sin2, jnp.zeros((seq, pad), F32)], axis=1)
    return cos2, sin2


def _moe(x1, xp1, logits_t, w1, w3, w2, layer, batch, seq):
    n_e = w1.shape[1]
    cap = CAPACITY_FACTOR * seq // n_e
    idx_col, gate_col = _route(logits_t, batch, seq, cap)
    idx_be = idx_col.reshape(batch * n_e, 1, cap)
    idx_eb = idx_col.reshape(batch, n_e, cap).transpose(1, 0, 2).reshape(n_e * batch, 1, cap)
    gate_eb = gate_col.reshape(batch, n_e, cap).transpose(1, 0, 2).reshape(n_e * batch * cap, 1)
    y = _ffn(idx_eb, xp1, w1, w3, w2, layer, gate_eb, batch, seq, cap, tf=256)
    return _combine(idx_be, y, x1, batch, seq, cap, n_e)


def _split_router(w_router):
    wr = w_router.T
    hi = wr.astype(BF16)
    lo = (wr - hi.astype(F32)).astype(BF16)
    return hi, lo


def _even_mixer(xb, batch, seq, w_in, q_norm, w_uq, kv_norm, w_ukv, gate_w2, gate_b, gla_norm,
                cos_r, sin_r):
    d = w_in.shape[0]
    o_cq, o_ckv, o_kr = 0, MLA_Q_RANK, MLA_Q_RANK + MLA_KV_RANK
    o_gq = o_kr + MLA_ROPE_DIM
    o_gk = o_gq + GLA_HEADS * GLA_DK
    o_gv = o_gk + GLA_HEADS * GLA_DK
    o_gr = o_gv + GLA_HEADS * GLA_DV
    o_lat = o_gr + GLA_HEADS * GLA_DV
    o_end = o_lat + 2 * GLA_GATE_RANK
    zeros = lambda n: jnp.zeros((d, n), BF16)
    tail_pad = LANES - MLA_ROPE_DIM - 2 * GLA_GATE_RANK
    w_in = w_in.astype(BF16)
    w_in_l = jnp.concatenate([
        w_in[:, o_cq:o_ckv], w_in[:, o_gq:o_gk], w_in[:, o_gk:o_gv], w_in[:, o_ckv:o_kr],
        w_in[:, o_kr:o_gq], w_in[:, o_lat:o_end], zeros(tail_pad), zeros(H_GV - H_TAIL - LANES),
        w_in[:, o_gv:o_gr], w_in[:, o_gr:o_lat]], axis=1)
    h = _matmul(xb, w_in_l, F32, tm=1024, tn=1024)

    qk = MLA_NOPE_DIM + MLA_ROPE_DIM
    wuq_l = jnp.pad(w_uq.reshape(MLA_Q_RANK, MLA_HEADS, qk),
                    ((0, 0), (0, 0), (0, MLA_QK_PAD - qk))).reshape(MLA_Q_RANK, MLA_HEADS * MLA_QK_PAD)
    wukv3 = w_ukv.reshape(MLA_KV_RANK, MLA_HEADS, MLA_NOPE_DIM + MLA_V_DIM)
    wukv_l = jnp.concatenate([wukv3[:, :, :MLA_NOPE_DIM].reshape(MLA_KV_RANK, -1),
                              wukv3[:, :, MLA_NOPE_DIM:].reshape(MLA_KV_RANK, -1)], axis=1)
    gw = GLA_HEADS * GLA_DK
    wg = jnp.zeros((LANES, 2 * gw), F32)
    wg = wg.at[MLA_ROPE_DIM:MLA_ROPE_DIM + GLA_GATE_RANK, :gw].set(gate_w2[0])
    wg = wg.at[MLA_ROPE_DIM + GLA_GATE_RANK:MLA_ROPE_DIM + 2 * GLA_GATE_RANK, gw:].set(gate_w2[1])
    gb = jnp.concatenate([gate_b[0], gate_b[1]])[None, :]
    q_p, k_p, v_p, la = _mla_prep(h, q_norm[None, :], wuq_l.astype(BF16), kv_norm[None, :],
                                  wukv_l.astype(BF16), wg.astype(BF16), gb, cos_r, sin_r, seq, tm=256)
    o_mla = _attention(q_p, k_p, v_p, batch=batch, seq=seq, kv_heads=MLA_HEADS, group=1,
                       dqk=MLA_QK_PAD, dv=MLA_V_DIM, q_off=0, k_off=0, v_off=0, tq=2048)
    o_gla = _gla(h, la, gla_norm.reshape(GLA_HEADS, 1, GLA_DV), batch, seq)
    return o_mla, o_gla


def _odd_mixer(xb, batch, seq, w_qkv, q_norm, k_norm, cos_g, sin_g):
    hd = GQA_HEAD_DIM
    scale = hd ** -0.5 * LOG2_E
    n_qk, n_v = (GQA_HEADS + GQA_KV_HEADS) * hd, GQA_KV_HEADS * hd
    split = jnp.concatenate([jnp.arange(0, hd, 2), jnp.arange(1, hd, 2)])
    d_in = w_qkv.shape[0]
    w_qk = w_qkv[:, :n_qk].reshape(d_in, n_qk // hd, hd // 2, 2).transpose(0, 1, 3, 2).reshape(d_in, n_qk)
    w_l = jnp.concatenate([w_qk, w_qkv[:, n_qk:]], axis=1)
    gain_row = jnp.concatenate([jnp.tile(q_norm[split] * scale, GQA_HEADS), jnp.tile(k_norm[split], GQA_KV_HEADS),
                                jnp.ones((n_v,), F32)]).reshape(-1, 1, 4 * hd)
    normed_row = jnp.concatenate([jnp.ones((n_qk,), F32), jnp.zeros((n_v,), F32)]).reshape(-1, 1, 4 * hd)
    gains = jnp.concatenate([gain_row, normed_row], axis=1)
    qkv = _qkv_proj(xb, w_l.astype(BF16), gains, cos_g, sin_g, seq, tm=1024)
    group = GQA_HEADS // GQA_KV_HEADS
    return _attention(qkv, qkv, qkv, batch=batch, seq=seq, kv_heads=GQA_KV_HEADS, group=group,
                      dqk=hd, dv=hd, q_off=0, k_off=GQA_HEADS, v_off=GQA_HEADS + GQA_KV_HEADS, tq=512)


def kernel(x, mix_w_in, mla_q_norm, mla_w_uq, mla_kv_norm, mla_w_ukv, gla_gate_w2, gla_gate_b,
           gla_out_norm, mix_w_out, gqa_w_qkv, gqa_q_norm, gqa_k_norm, gqa_w_out,
           moe_router, moe_w1, moe_w3, moe_w2, ln_mix_g, ln_mix_b, ln_ffn_g, ln_ffn_b):
    batch, seq, d = x.shape
    cos_r, sin_r = _rope_tables(seq, MLA_ROPE_DIM)
    cos_g, sin_g = _rope_tables(seq, GQA_HEAD_DIM, split_halves=True)
    xf = x.reshape(batch * seq, d)
    xb = xf.astype(BF16)
    n_layers = moe_router.shape[0]
    for layer in range(n_layers):
        i = layer // 2
        wr_hi, wr_lo = _split_router(moe_router[layer])
        g_mix, b_mix = ln_mix_g[layer][None, :], ln_mix_b[layer][None, :]
        if layer % 2 == 0:
            o_mla, o_gla = _even_mixer(xb, batch, seq, mix_w_in[i], mla_q_norm[i], mla_w_uq[i],
                                       mla_kv_norm[i], mla_w_ukv[i], gla_gate_w2[i], gla_gate_b[i],
                                       gla_out_norm[i], cos_r, sin_r)
            w_out = mix_w_out[i].astype(BF16)
            n_mla = MLA_HEADS * MLA_V_DIM
            x1, xp1, logits_t = _proj_ln([o_mla, o_gla], [w_out[:n_mla], w_out[n_mla:]], xf,
                                         g_mix, b_mix, wr_hi, wr_lo, tm=512)
        else:
            o = _odd_mixer(xb, batch, seq, gqa_w_qkv[i], gqa_q_norm[i], gqa_k_norm[i], cos_g, sin_g)
            x1, xp1, logits_t = _proj_ln([o], [gqa_w_out[i].astype(BF16)], xf,
                                         g_mix, b_mix, wr_hi, wr_lo, tm=512)
        z = _moe(x1, xp1, logits_t, moe_w1, moe_w3, moe_w2, layer, batch, seq)
        xf, xb = _ln(z, ln_ffn_g[layer][None, :], ln_ffn_b[layer][None, :], tm=512)
    return xf.reshape(batch, seq, d)
```

```python
import functools

import jax
import jax.numpy as jnp
from jax import lax
from jax.experimental import pallas as pl
from jax.experimental.pallas import tpu as pltpu

F32 = jnp.float32
BF16 = jnp.bfloat16

GRID_W = 64
ROPE_THETA = 10000.0
LN_EPS = 1e-5
RMS_EPS = 1e-6
DEPTH = 4
DEEPNORM_ALPHA = (2.0 * DEPTH) ** 0.25

MLA_HEADS = 8
MLA_Q_RANK = 512
MLA_KV_RANK = 256
MLA_NOPE_DIM = 128
MLA_ROPE_DIM = 64
MLA_V_DIM = 128
MLA_QK_PAD = 256

GLA_HEADS = 4
GLA_DK = 128
GLA_DV = 256
GLA_GATE_RANK = 16
GLA_GATE_TAU = 16.0
GLA_CHUNK = 64
GLA_GROUP = 256
GLA_SCAN_UNROLL = 8
GLA_GROUP_UNROLL = 4

GQA_HEADS = 16
GQA_KV_HEADS = 4
GQA_HEAD_DIM = 128

N_EXPERTS = 16
CAPACITY_FACTOR = 2

LANES = 128
VMEM_LIMIT_BYTES = 56 * 1024 * 1024

LOG2_E = 1.4426950408889634
ATTN_SAFE_LOG2_RANGE = 60.0

H_CQ = 0
H_GQ = 512
H_GK = 1024
H_CKV = 1536
H_TAIL = 1792
H_GV = 2048
H_GR = 3072
H_WIDTH = 4096


def _cparams(sem):
    return pltpu.CompilerParams(dimension_semantics=sem, vmem_limit_bytes=VMEM_LIMIT_BYTES)


def _nt_dot(a, b):
    return lax.dot_general(a, b, (((1,), (1,)), ((), ())), preferred_element_type=F32)


def _tn_dot(a, b):
    return lax.dot_general(a, b, (((0,), (0,)), ((), ())), preferred_element_type=F32)


def _rope(x, cos, sin_signed):
    lane = lax.broadcasted_iota(jnp.int32, x.shape, 1)
    partner = jnp.where((lane & 1) == 0, pltpu.roll(x, LANES - 1, 1), pltpu.roll(x, 1, 1))
    return x * cos + partner * sin_signed


def _rms(x, gain):
    return x * lax.rsqrt(jnp.mean(x * x, axis=-1, keepdims=True) + RMS_EPS) * gain


def _store_token_major(ref, val, first_row=0):
    count, n = val.shape[0], val.shape[1] // LANES
    for j in range(n):
        ref[pl.ds(first_row * n + j, count, stride=n), :] = val[:, j * LANES:(j + 1) * LANES]


def _layer_norm(z, g, b):
    mu = jnp.mean(z, axis=-1, keepdims=True)
    zc = z - mu
    var = jnp.mean(zc * zc, axis=-1, keepdims=True)
    return zc * lax.rsqrt(var + LN_EPS) * g + b


def _mm_kernel(x_ref, w_ref, o_ref):
    o_ref[...] = jnp.dot(x_ref[...], w_ref[...], preferred_element_type=F32).astype(o_ref.dtype)


def _matmul(x, w, out_dtype, tm, tn):
    m, k = x.shape
    n = w.shape[1]
    tm, tn = min(tm, m), min(tn, n)
    return pl.pallas_call(
        _mm_kernel,
        grid=(n // tn, m // tm),
        in_specs=[pl.BlockSpec((tm, k), lambda j, i: (i, 0)),
                  pl.BlockSpec((k, tn), lambda j, i: (0, j))],
        out_specs=pl.BlockSpec((tm, tn), lambda j, i: (i, j)),
        out_shape=jax.ShapeDtypeStruct((m, n), out_dtype),
        compiler_params=_cparams(("parallel", "parallel")),
        name="dense_matmul",
    )(x, w)


def _qkv_kernel(x_ref, w_ref, g_ref, ones_ref, cos_ref, sin_ref, o_ref, *, heads_per_tile):
    acc = jnp.dot(x_ref[...], w_ref[...], preferred_element_type=F32)
    sq = acc * acc
    sq_hi = sq.astype(BF16)
    sq_lo = (sq - sq_hi.astype(F32)).astype(BF16)
    ones_bd = ones_ref[...]
    ms = (jnp.dot(sq_hi, ones_bd, preferred_element_type=F32)
          + jnp.dot(sq_lo, ones_bd, preferred_element_type=F32)) * (1.0 / GQA_HEAD_DIM)
    y = acc * lax.rsqrt(ms + RMS_EPS) * g_ref[0:1, :]
    cos = cos_ref[...]
    sin = sin_ref[...]
    for hd in range(heads_per_tile):
        sl = slice(hd * GQA_HEAD_DIM, (hd + 1) * GQA_HEAD_DIM)
        yh = y[:, sl]
        rot = yh * cos + pltpu.roll(yh, GQA_HEAD_DIM // 2, 1) * sin
        o_ref[:, sl] = jnp.where(g_ref[1:2, sl] > 0.5, rot, acc[:, sl]).astype(o_ref.dtype)


def _qkv_proj(xb, w, gains, cos, sin, seq, tm):
    m, k = xb.shape
    n = w.shape[1]
    tn = 4 * GQA_HEAD_DIM
    tm = min(tm, seq)
    n_tiles = n // tn
    pos_blocks = seq // tm
    kern = functools.partial(_qkv_kernel, heads_per_tile=4)
    col_head = jnp.arange(tn, dtype=jnp.int32) // GQA_HEAD_DIM
    ones_bd = (col_head[:, None] == col_head[None, :]).astype(BF16)
    return pl.pallas_call(
        kern,
        grid=(n_tiles, m // tm),
        in_specs=[pl.BlockSpec((tm, k), lambda j, i: (i, 0)),
                  pl.BlockSpec((k, tn), lambda j, i: (0, j)),
                  pl.BlockSpec((None, 2, tn), lambda j, i: (j, 0, 0)),
                  pl.BlockSpec((tn, tn), lambda j, i: (0, 0)),
                  pl.BlockSpec((tm, LANES), lambda j, i: (i % pos_blocks, 0)),
                  pl.BlockSpec((tm, LANES), lambda j, i: (i % pos_blocks, 0))],
        out_specs=pl.BlockSpec((tm, tn), lambda j, i: (i, j)),
        out_shape=jax.ShapeDtypeStruct((m, n), BF16),
        compiler_params=_cparams(("parallel", "parallel")),
        name="gqa_qkv_proj",
    )(xb, w, gains, ones_bd, cos, sin)


def _attn_kernel(q_ref, k_ref, v_ref, o_ref, vx_s, kmax_s, *, group, dqk, dv, kc_single, kc_online):
    tq = q_ref.shape[0]
    seq = k_ref.shape[0]
    rows = group * tq

    @pl.when(pl.program_id(2) == 0)
    def _():
        lane = lax.broadcasted_iota(jnp.int32, (seq, LANES), 1)
        vx_s[:, :dv] = v_ref[...]
        vx_s[:, dv:] = jnp.where(lane == 0, 1.0, 0.0).astype(BF16)
        k32 = k_ref[...].astype(F32)
        k_sq = jnp.sum(k32 * k32, axis=1, keepdims=True)
        kmax_s[...] = jnp.max(k_sq, axis=0, keepdims=True)

    q = jnp.concatenate([q_ref[:, g * dqk:(g + 1) * dqk] for g in range(group)], axis=0)
    q32 = q.astype(F32)
    bound = jnp.sqrt(jnp.sum(q32 * q32, axis=1, keepdims=True) * kmax_s[...])
    single_pass = jnp.max(bound) <= ATTN_SAFE_LOG2_RANGE

    def write(acc):
        o = acc[:, :dv] / acc[:, dv:dv + 1]
        for g in range(group):
            o_ref[:, g * dv:(g + 1) * dv] = o[g * tq:(g + 1) * tq, :].astype(o_ref.dtype)

    @pl.when(single_pass)
    def _():
        kc = kc_single
        acc = jnp.zeros((rows, dv + LANES), F32)
        for c in range(seq // kc):
            p = jnp.exp2(_nt_dot(q, k_ref[c * kc:(c + 1) * kc, :]) - bound)
            acc = acc + jnp.dot(p.astype(BF16), vx_s[c * kc:(c + 1) * kc, :], preferred_element_type=F32)
        write(acc)

    @pl.when(jnp.logical_not(single_pass))
    def _():
        kc = kc_online
        m = jnp.full((rows, 1), -jnp.inf, F32)
        acc = jnp.zeros((rows, dv + LANES), F32)
        for c in range(seq // kc):
            s = _nt_dot(q, k_ref[c * kc:(c + 1) * kc, :])
            m_new = jnp.maximum(m, jnp.max(s, axis=1, keepdims=True))
            p = jnp.exp2(s - m_new)
            acc = acc * jnp.exp2(m - m_new) + jnp.dot(p.astype(BF16), vx_s[c * kc:(c + 1) * kc, :],
                                                      preferred_element_type=F32)
            m = m_new
        write(acc)


def _attention(q_arr, k_arr, v_arr, *, batch, seq, kv_heads, group, dqk, dv,
               q_off, k_off, v_off, tq):
    tq = min(tq, seq)
    nq = seq // tq
    kern = functools.partial(_attn_kernel, group=group, dqk=dqk, dv=dv,
                             kc_single=min(256, seq), kc_online=min(1024, seq))
    return pl.pallas_call(
        kern,
        grid=(batch, kv_heads, nq),
        in_specs=[pl.BlockSpec((tq, group * dqk), lambda b, h, i: (b * nq + i, q_off + h)),
                  pl.BlockSpec((seq, dqk), lambda b, h, i: (b, k_off + h)),
                  pl.BlockSpec((seq, dv), lambda b, h, i: (b, v_off + h))],
        out_specs=pl.BlockSpec((tq, group * dv), lambda b, h, i: (b * nq + i, h)),
        out_shape=jax.ShapeDtypeStruct((batch * seq, kv_heads * group * dv), BF16),
        scratch_shapes=[pltpu.VMEM((seq, dv + LANES), BF16), pltpu.VMEM((1, 1), F32)],
        compiler_params=_cparams(("arbitrary", "arbitrary", "arbitrary")),
        name="softmax_attention",
    )(q_arr, k_arr, v_arr)


def _mla_prep_kernel(cq_ref, ckv_ref, tail_ref, qn_ref, wuq_ref, kvn_ref, wukv_ref, wg_ref, gb_ref,
                     cos_ref, sin_ref, q_ref, k_ref, v_ref, la_ref):
    cos = cos_ref[...]
    sin = sin_ref[...]
    scale = (MLA_NOPE_DIM + MLA_ROPE_DIM) ** -0.5 * LOG2_E
    q = jnp.dot(_rms(cq_ref[...], qn_ref[...]).astype(BF16), wuq_ref[...],
                preferred_element_type=F32) * scale
    kv = jnp.dot(_rms(ckv_ref[...], kvn_ref[...]).astype(BF16), wukv_ref[...],
                 preferred_element_type=F32)
    tail = tail_ref[...]
    lane = lax.broadcasted_iota(jnp.int32, tail.shape, 1)
    k_pe = jnp.where(lane < MLA_ROPE_DIM, _rope(tail, cos, sin), 0.0).astype(BF16)
    for hd in range(MLA_HEADS):
        c0 = hd * MLA_QK_PAD
        q_ref[:, c0:c0 + LANES] = q[:, c0:c0 + LANES].astype(BF16)
        q_ref[:, c0 + LANES:c0 + 2 * LANES] = _rope(q[:, c0 + LANES:c0 + 2 * LANES], cos, sin).astype(BF16)
        k_ref[:, c0:c0 + LANES] = kv[:, hd * LANES:(hd + 1) * LANES].astype(BF16)
        k_ref[:, c0 + LANES:c0 + 2 * LANES] = k_pe
    nv = MLA_HEADS * MLA_V_DIM
    v_ref[...] = kv[:, nv:].astype(BF16)
    gate = jnp.dot(tail.astype(BF16), wg_ref[...], preferred_element_type=F32) + gb_ref[...]
    log_sig = jnp.minimum(gate, 0.0) - jnp.log1p(jnp.exp(-jnp.abs(gate)))
    la_ref[...] = log_sig / GLA_GATE_TAU


def _mla_prep(h, qn, wuq, kvn, wukv, wg, gb, cos, sin, seq, tm):
    m = h.shape[0]
    tm = min(tm, seq)
    pos_blocks = seq // tm
    qw = MLA_HEADS * MLA_QK_PAD
    vw = MLA_HEADS * MLA_V_DIM
    gw = 2 * GLA_HEADS * GLA_DK
    const = lambda i: (0, 0)
    return pl.pallas_call(
        _mla_prep_kernel,
        grid=(m // tm,),
        in_specs=[pl.BlockSpec((tm, MLA_Q_RANK), lambda i: (i, H_CQ // MLA_Q_RANK)),
                  pl.BlockSpec((tm, MLA_KV_RANK), lambda i: (i, H_CKV // MLA_KV_RANK)),
                  pl.BlockSpec((tm, LANES), lambda i: (i, H_TAIL // LANES)),
                  pl.BlockSpec((1, MLA_Q_RANK), const),
                  pl.BlockSpec((MLA_Q_RANK, qw), const),
                  pl.BlockSpec((1, MLA_KV_RANK), const),
                  pl.BlockSpec((MLA_KV_RANK, 2 * vw), const),
                  pl.BlockSpec((LANES, gw), const),
                  pl.BlockSpec((1, gw), const),
                  pl.BlockSpec((tm, LANES), lambda i: (i % pos_blocks, 0)),
                  pl.BlockSpec((tm, LANES), lambda i: (i % pos_blocks, 0))],
        out_specs=[pl.BlockSpec((tm, qw), lambda i: (i, 0)),
                   pl.BlockSpec((tm, qw), lambda i: (i, 0)),
                   pl.BlockSpec((tm, vw), lambda i: (i, 0)),
                   pl.BlockSpec((tm, gw), lambda i: (i, 0))],
        out_shape=[jax.ShapeDtypeStruct((m, qw), BF16),
                   jax.ShapeDtypeStruct((m, qw), BF16),
                   jax.ShapeDtypeStruct((m, vw), BF16),
                   jax.ShapeDtypeStruct((m, gw), F32)],
        compiler_params=_cparams(("parallel",)),
        name="mla_prep",
    )(h, h, h, qn, wuq, kvn, wukv, wg, gb, cos, sin)


def _split3(x):
    hi = x.astype(BF16)
    r1 = x - hi.astype(F32)
    mid = r1.astype(BF16)
    lo = (r1 - mid.astype(F32)).astype(BF16)
    return hi, mid, lo


def _gla_kernel(q_ref, k_ref, v_ref, gr_ref, laf_ref, lab_ref, gn_ref, o_ref,
                of_s, ob_s, qf_s, qb_s, kf_s, kb_s, df_s, db_s, stf_s, stb_s):
    seq = q_ref.shape[0]
    c = GLA_CHUNK
    r = min(GLA_GROUP, seq)
    cpg = r // c
    n_groups = seq // r
    group_unroll = min(GLA_GROUP_UNROLL, n_groups)
    scan_unroll = min(GLA_SCAN_UNROLL, seq // c)
    n_chunks = seq // c
    scale = GLA_DK ** -0.5

    row = lax.broadcasted_iota(jnp.int32, (r, r), 0)
    col = lax.broadcasted_iota(jnp.int32, (r, r), 1)
    same = (row // c) == (col // c)
    tril = same & (col <= row)
    triu = same & (col >= row)
    tril_b = tril.astype(F32).astype(BF16)
    triu_b = triu.astype(F32).astype(BF16)

    def group_pair(gp, carry):
        for u in range(group_unroll):
            group_body(gp * group_unroll + u)
        return carry

    def group_body(gi):
        r0 = pl.multiple_of(gi * r, r)
        rows = pl.ds(r0, r)
        q = q_ref[rows, :] * scale
        k = k_ref[rows, :]
        v = v_ref[rows, :].astype(BF16)
        for la_ref, mask, mask_b, edge, o_s, q_s, k_s, d_s in (
                (laf_ref, tril, tril_b, c - 1, of_s, qf_s, kf_s, df_s),
                (lab_ref, triu, triu_b, 0, ob_s, qb_s, kb_s, db_s)):
            hi, mid, lo = _split3(la_ref[rows, :])
            b = (jnp.dot(mask_b, hi, preferred_element_type=F32)
                 + jnp.dot(mask_b, mid, preferred_element_type=F32)
                 + jnp.dot(mask_b, lo, preferred_element_type=F32))
            b_edge = jnp.concatenate(
                [jnp.broadcast_to(b[ci * c + edge:ci * c + edge + 1, :], (c, GLA_DK)) for ci in range(cpg)],
                axis=0)
            q_in = (q * jnp.exp(b)).astype(BF16)
            k_in = (k * jnp.exp(-b)).astype(BF16)
            k_st = (k * jnp.exp(b_edge - b)).astype(BF16)
            att = jnp.where(mask, _nt_dot(q_in, k_in), 0.0)
            o_s[rows, :] = jnp.dot(att.astype(BF16), v, preferred_element_type=F32)
            q_s[rows, :] = q_in
            k_s[rows, :] = k_st
            for ci in range(cpg):
                d_s[pl.ds(gi * cpg + ci, 1), :] = jnp.exp(b[ci * c + edge:ci * c + edge + 1, :])

    lax.fori_loop(0, n_groups // group_unroll, group_pair, 0)

    stf_s[...] = jnp.zeros_like(stf_s)
    stb_s[...] = jnp.zeros_like(stb_s)

    def chunk_body(i, carry):
        for u in range(scan_unroll):
            fwd_n = i * scan_unroll + u
            for n, o_s, q_s, k_s, d_s, st_s in ((fwd_n, of_s, qf_s, kf_s, df_s, stf_s),
                                                (n_chunks - 1 - fwd_n, ob_s, qb_s, kb_s, db_s, stb_s)):
                rows = pl.ds(pl.multiple_of(n * c, c), c)
                st = st_s[...]
                o_s[rows, :] += _nt_dot(q_s[rows, :], st.astype(BF16))
                ds = _tn_dot(v_ref[rows, :].astype(BF16), k_s[rows, :])
                st_s[...] = d_s[pl.ds(n, 1), :] * st + ds
        return carry

    lax.fori_loop(0, n_chunks // scan_unroll, chunk_body, 0)

    gain = gn_ref[...]

    def out_body(gi, carry):
        rows = pl.ds(pl.multiple_of(gi * r, r), r)
        o = _rms(of_s[rows, :] + ob_s[rows, :], gain)
        gr = gr_ref[rows, :]
        o_ref[rows, :] = (o * (gr * jax.nn.sigmoid(gr))).astype(o_ref.dtype)
        return carry

    lax.fori_loop(0, n_groups, out_body, 0)


def _gla(h, la, gn, batch, seq):
    dk, dv, nh = GLA_DK, GLA_DV, GLA_HEADS
    n_chunks = seq // GLA_CHUNK
    return pl.pallas_call(
        _gla_kernel,
        grid=(batch, nh),
        in_specs=[pl.BlockSpec((seq, dk), lambda b, hd: (b, H_GQ // dk + hd)),
                  pl.BlockSpec((seq, dk), lambda b, hd: (b, H_GK // dk + hd)),
                  pl.BlockSpec((seq, dv), lambda b, hd: (b, H_GV // dv + hd)),
                  pl.BlockSpec((seq, dv), lambda b, hd: (b, H_GR // dv + hd)),
                  pl.BlockSpec((seq, dk), lambda b, hd: (b, hd)),
                  pl.BlockSpec((seq, dk), lambda b, hd: (b, nh + hd)),
                  pl.BlockSpec((None, 1, dv), lambda b, hd: (hd, 0, 0))],
        out_specs=pl.BlockSpec((seq, dv), lambda b, hd: (b, hd)),
        out_shape=jax.ShapeDtypeStruct((batch * seq, nh * dv), BF16),
        scratch_shapes=[pltpu.VMEM((seq, dv), F32), pltpu.VMEM((seq, dv), F32),
                        pltpu.VMEM((seq, dk), BF16), pltpu.VMEM((seq, dk), BF16),
                        pltpu.VMEM((seq, dk), BF16), pltpu.VMEM((seq, dk), BF16),
                        pltpu.VMEM((n_chunks, dk), F32), pltpu.VMEM((n_chunks, dk), F32),
                        pltpu.VMEM((dv, dk), F32), pltpu.VMEM((dv, dk), F32)],
        compiler_params=_cparams(("arbitrary", "arbitrary")),
        name="gla_bidirectional",
    )(h, h, h, h, la, la, gn)


PROJ_SUBTILES = 1


def _proj_ln_kernel(*refs, n_in):
    a_refs = refs[:n_in]
    w_refs = refs[n_in:2 * n_in]
    x_ref, g_ref, b_ref, wrh_ref, wrl_ref, xo_ref, xp_ref, lg_ref = refs[2 * n_in:]
    tm, d = x_ref.shape
    half = d // 2
    sub = tm // PROJ_SUBTILES
    wrh = wrh_ref[...]
    for t in range(PROJ_SUBTILES):
        rows = slice(t * sub, (t + 1) * sub)
        mix = jnp.dot(a_refs[0][rows, :], w_refs[0][...], preferred_element_type=F32)
        for a_ref, w_ref in zip(a_refs[1:], w_refs[1:]):
            mix = mix + jnp.dot(a_ref[rows, :], w_ref[...], preferred_element_type=F32)
        y = _layer_norm(DEEPNORM_ALPHA * x_ref[rows, :] + mix, g_ref[...], b_ref[...])
        xo_ref[rows, :] = y
        y_hi = y.astype(BF16)
        y_hi32 = y_hi.astype(F32)
        bits = lax.bitcast_convert_type(y_hi32, jnp.int32)
        _store_token_major(xp_ref, lax.shift_right_logical(bits[:, :half], 16) | bits[:, half:],
                           first_row=t * sub)
        y_lo = (y - y_hi32).astype(BF16)
        lg_ref[:, rows] = _nt_dot(wrh, y_hi) + _nt_dot(wrh, y_lo) + _nt_dot(wrl_ref[...], y_hi)


def _proj_ln(acts, weights, x, g, b, wr_hi, wr_lo, tm):
    m, d = x.shape
    tm = min(tm, m)
    n_in = len(acts)
    n_e = wr_hi.shape[0]
    np_ = d // (2 * LANES)
    const = lambda i: (0, 0)
    in_specs = ([pl.BlockSpec((tm, a.shape[1]), lambda i: (i, 0)) for a in acts]
                + [pl.BlockSpec(w.shape, const) for w in weights]
                + [pl.BlockSpec((tm, d), lambda i: (i, 0)),
                   pl.BlockSpec((1, d), const), pl.BlockSpec((1, d), const),
                   pl.BlockSpec((n_e, d), const), pl.BlockSpec((n_e, d), const)])
    return pl.pallas_call(
        functools.partial(_proj_ln_kernel, n_in=n_in),
        grid=(m // tm,),
        in_specs=in_specs,
        out_specs=[pl.BlockSpec((tm, d), lambda i: (i, 0)),
                   pl.BlockSpec((tm * np_, LANES), lambda i: (i, 0)),
                   pl.BlockSpec((n_e, tm), lambda i: (0, i))],
        out_shape=[jax.ShapeDtypeStruct((m, d), F32),
                   jax.ShapeDtypeStruct((m * np_, LANES), jnp.int32),
                   jax.ShapeDtypeStruct((n_e, m), F32)],
        compiler_params=_cparams(("parallel",)),
        name="out_proj_layernorm_router",
    )(*acts, *weights, x, g, b, wr_hi, wr_lo)


def _route_kernel(lg_ref, idx_ref, gate_ref, aff_s, cum_s, start_s, *, cap):
    lg = lg_ref[...]
    ex = jnp.exp(lg - jnp.max(lg, axis=0, keepdims=True))
    aff = ex / jnp.sum(ex, axis=0, keepdims=True)
    bits = lax.bitcast_convert_type(aff, jnp.int32)
    n_e, seq = lg.shape
    nblk = seq // LANES
    capf = float(cap)

    def bisect(_, carry):
        lo, hi = carry
        mid = lo + ((hi - lo + 1) >> 1)
        cnt = jnp.sum(jnp.where(bits >= mid, 1.0, 0.0), axis=1, keepdims=True)
        ok = cnt >= capf
        return jnp.where(ok, mid, lo), jnp.where(ok, hi, mid - 1)

    lo0 = jnp.zeros((n_e, 1), jnp.int32)
    hi0 = jnp.full((n_e, 1), 0x7F800000, jnp.int32)
    thr, _ = lax.fori_loop(0, 32, bisect, (lo0, hi0))

    gt = bits > thr
    eq = bits == thr
    need = capf - jnp.sum(jnp.where(gt, 1.0, 0.0), axis=1, keepdims=True)

    r_i = lax.broadcasted_iota(jnp.int32, (LANES, LANES), 0)
    c_i = lax.broadcasted_iota(jnp.int32, (LANES, LANES), 1)
    upper = (r_i <= c_i).astype(F32).astype(BF16)
    eq_off = jnp.zeros((n_e, 1), F32)
    sel_off = jnp.zeros((n_e, 1), F32)
    for j in range(nblk):
        blk = slice(j * LANES, (j + 1) * LANES)
        eq01 = jnp.where(eq[:, blk], 1.0, 0.0)
        eq_cum = jnp.dot(eq01.astype(BF16), upper, preferred_element_type=F32) + eq_off
        eq_off = eq_cum[:, LANES - 1:LANES]
        tie_rank = eq_cum - eq01
        sel01 = jnp.where(gt[:, blk] | (eq[:, blk] & (tie_rank < need)), 1.0, 0.0)
        sel_cum = jnp.dot(sel01.astype(BF16), upper, preferred_element_type=F32) + sel_off
        rows = slice(j * n_e, (j + 1) * n_e)
        cum_s[rows, :] = sel_cum
        start_s[rows, :] = jnp.broadcast_to(sel_off, (n_e, LANES))
        aff_s[rows, :] = aff[:, blk]
        sel_off = sel_cum[:, LANES - 1:LANES]

    slot_row = lax.broadcasted_iota(jnp.int32, (1, cap), 1).astype(F32)
    slot_col = lax.broadcasted_iota(jnp.int32, (cap, 1), 0).astype(F32)
    lane = lax.broadcasted_iota(jnp.int32, (cap, LANES), 1).astype(F32)
    blk_id = lax.broadcasted_iota(jnp.int32, (nblk, LANES), 0).astype(F32).astype(BF16)

    def extract(e, carry):
        rows = pl.ds(e, nblk, stride=n_e)
        cum_e = cum_s[rows, :]
        start_e = start_s[rows, :][:, 0:1]
        end_e = cum_e[:, LANES - 1:LANES]
        in_blk = jnp.where((start_e <= slot_row) & (slot_row < end_e), 1.0, 0.0).astype(BF16)
        cum_hi = jnp.floor(cum_e * (1.0 / 32.0))
        cum_lo = cum_e - 32.0 * cum_hi
        g = 32.0 * _tn_dot(in_blk, cum_hi.astype(BF16)) + _tn_dot(in_blk, cum_lo.astype(BF16))
        blk_of_slot = _tn_dot(in_blk, blk_id)[:, 0:1]
        pos = jnp.sum(jnp.where(g <= slot_col, 1.0, 0.0), axis=1, keepdims=True)
        a_hi, a_mid, a_lo = _split3(aff_s[rows, :])
        a = (_tn_dot(in_blk, a_hi) + _tn_dot(in_blk, a_mid)) + _tn_dot(in_blk, a_lo)
        gate = jnp.sum(jnp.where(lane == pos, a, 0.0), axis=1, keepdims=True)
        out_rows = pl.ds(pl.multiple_of(e * cap, cap), cap)
        idx_ref[out_rows, :] = (float(LANES) * blk_of_slot + pos).astype(jnp.int32)
        gate_ref[out_rows, :] = gate
        return carry

    lax.fori_loop(0, n_e, extract, 0)


def _route(logits_t, batch, seq, cap):
    n_e = logits_t.shape[0]
    kern = functools.partial(_route_kernel, cap=cap)
    scratch_rows = (seq // LANES) * n_e
    return pl.pallas_call(
        kern,
        grid=(batch,),
        in_specs=[pl.BlockSpec((n_e, seq), lambda b: (0, b))],
        out_specs=[pl.BlockSpec((n_e * cap, 1), lambda b: (b, 0)),
                   pl.BlockSpec((n_e * cap, 1), lambda b: (b, 0))],
        out_shape=[jax.ShapeDtypeStruct((batch * n_e * cap, 1), jnp.int32),
                   jax.ShapeDtypeStruct((batch * n_e * cap, 1), F32)],
        scratch_shapes=[pltpu.VMEM((scratch_rows, LANES), F32), pltpu.VMEM((scratch_rows, LANES), F32),
                        pltpu.VMEM((scratch_rows, LANES), F32)],
        compiler_params=_cparams(("parallel",)),
        name="expert_choice_route",
    )(logits_t)


GATHER_UNROLL = 32


def _ffn_kernel(idx_cur, idx_nxt, xp_hbm, w1_ref, w3_ref, w2_ref, gate_ref, o_ref,
                land, x_lo, x_hi, sem, *, cap, seq, bpt, issue_steps, unpack_rows):
    m_tiles = pl.num_programs(1)
    f = pl.program_id(2)
    t = pl.program_id(0) * m_tiles + pl.program_id(1)
    n_tiles = pl.num_programs(0) * m_tiles
    slot = t % 2
    tm = bpt * cap
    half = x_lo.shape[1]
    npk = half // LANES

    def row_copy(dst_slot, r, src_row):
        return pltpu.make_async_copy(xp_hbm.at[pl.ds(pl.multiple_of(src_row * npk, npk), npk), :],
                                     land.at[dst_slot, pl.ds(pl.multiple_of(r * npk, npk), npk), :],
                                     sem.at[dst_slot])

    def issue(idx_ref, tile, dst_slot, bb, s0, n):
        src_base = ((tile % m_tiles) * bpt + bb) * seq
        dst_base = bb * cap

        def body(i, carry):
            for u in range(GATHER_UNROLL):
                s = s0 + i * GATHER_UNROLL + u
                row_copy(dst_slot, dst_base + s, src_base + idx_ref[bb, 0, s]).start()
            return carry

        lax.fori_loop(0, n // GATHER_UNROLL, body, 0)

    @pl.when((t == 0) & (f == 0))
    def _():
        for bb in range(bpt):
            issue(idx_cur, t, slot, bb, 0, cap)

    @pl.when(f == 0)
    def _():
        pltpu.make_async_copy(xp_hbm.at[pl.ds(0, tm * npk), :], land.at[slot], sem.at[slot]).wait()

        def unpack(i, carry):
            r0 = pl.multiple_of(i * unpack_rows, unpack_rows)
            rows = pl.ds(r0, unpack_rows)
            for j in range(npk):
                w = land[slot, pl.ds(r0 * npk + j, unpack_rows, stride=npk), :]
                cols = slice(j * LANES, (j + 1) * LANES)
                x_lo[rows, cols] = lax.bitcast_convert_type(w << 16, F32).astype(BF16)
                x_hi[rows, cols] = lax.bitcast_convert_type(w & jnp.int32(-65536), F32).astype(BF16)
            return carry

        lax.fori_loop(0, tm // unpack_rows, unpack, 0)
        o_ref[...] = jnp.zeros_like(o_ref)

    per_batch = issue_steps // bpt
    n = cap // per_batch
    for bb in range(bpt):
        @pl.when((f >= bb * per_batch) & (f < (bb + 1) * per_batch) & (t + 1 < n_tiles))
        def _(bb=bb):
            issue(idx_nxt, t + 1, 1 - slot, bb, (f - bb * per_batch) * n, n)

    w1 = w1_ref[...].astype(BF16)
    w3 = w3_ref[...].astype(BF16)
    lo, hi = x_lo[...], x_hi[...]
    h1 = (jnp.dot(lo, w1[:half], preferred_element_type=F32)
          + jnp.dot(hi, w1[half:], preferred_element_type=F32))
    h3 = (jnp.dot(lo, w3[:half], preferred_element_type=F32)
          + jnp.dot(hi, w3[half:], preferred_element_type=F32))
    hdn = (h1 * jax.nn.sigmoid(h1)) * h3
    part = jnp.dot(hdn.astype(BF16), w2_ref[...].astype(BF16), preferred_element_type=F32)
    is_last = f == pl.num_programs(2) - 1
    gate = jnp.where(is_last, gate_ref[...], 1.0)
    o_ref[...] = (o_ref[...] + part) * gate


def _ffn(idx_eb, xp, w1, w3, w2, layer, gate_col, batch, seq, cap, tf):
    _, n_e, d, ff = w1.shape
    bpt = min(2, batch)
    tm = bpt * cap
    mt = batch // bpt
    tf = min(tf, ff // bpt)
    nf = ff // tf
    assert nf >= 2 and ff % tf == 0
    issue_steps = bpt * min(2, nf // bpt)
    assert cap % (issue_steps // bpt * GATHER_UNROLL) == 0
    n_tiles = n_e * mt
    kern = functools.partial(_ffn_kernel, cap=cap, seq=seq, bpt=bpt, issue_steps=issue_steps,
                             unpack_rows=min(256, tm))
    smem_idx = lambda fn: pl.BlockSpec((bpt, 1, cap), fn, memory_space=pltpu.SMEM)
    return pl.pallas_call(
        kern,
        grid=(n_e, mt, nf),
        in_specs=[smem_idx(lambda e, m, f: (e * mt + m, 0, 0)),
                  smem_idx(lambda e, m, f: (jnp.minimum(e * mt + m + 1, n_tiles - 1), 0, 0)),
                  pl.BlockSpec(memory_space=pl.ANY),
                  pl.BlockSpec((None, None, d, tf), lambda e, m, f: (layer, e, 0, f)),
                  pl.BlockSpec((None, None, d, tf), lambda e, m, f: (layer, e, 0, f)),
                  pl.BlockSpec((None, None, tf, d), lambda e, m, f: (layer, e, f, 0)),
                  pl.BlockSpec((tm, 1), lambda e, m, f: (e * mt + m, 0))],
        out_specs=pl.BlockSpec((tm, d), lambda e, m, f: (e * mt + m, 0)),
        out_shape=jax.ShapeDtypeStruct((n_e * batch * cap, d), F32),
        scratch_shapes=[pltpu.VMEM((2, tm * (d // (2 * LANES)), LANES), jnp.int32),
                        pltpu.VMEM((tm, d // 2), BF16), pltpu.VMEM((tm, d // 2), BF16),
                        pltpu.SemaphoreType.DMA((2,))],
        compiler_params=_cparams(("arbitrary", "arbitrary", "arbitrary")),
        name="moe_expert_ffn",
    )(idx_eb, idx_eb, xp, w1, w3, w2, gate_col)


COMBINE_UNROLL = 8


def _combine_kernel(idx_ref, y_ref, x_hbm, g_ref, b_ref, xo_hbm, xb_hbm, acc, xb_stage, sem, out_sem,
                    *, cap, seq, n_e, row_block):
    b = pl.program_id(0)
    e = pl.program_id(1)

    @pl.when(e == 0)
    def _():
        cp = pltpu.make_async_copy(x_hbm.at[pl.ds(b * seq, seq), :], acc, sem)
        cp.start()
        cp.wait()

        def scale(i, carry):
            rows = pl.ds(pl.multiple_of(i * row_block, row_block), row_block)
            acc[rows, :] = acc[rows, :] * DEEPNORM_ALPHA
            return carry

        lax.fori_loop(0, seq // row_block, scale, 0)

    def add_rows(i, carry):
        s0 = i * COMBINE_UNROLL
        toks = [idx_ref[0, 0, s0 + u] for u in range(COMBINE_UNROLL)]
        sums = [acc[pl.ds(toks[u], 1), :] + y_ref[pl.ds(s0 + u, 1), :] for u in range(COMBINE_UNROLL)]
        for u in range(COMBINE_UNROLL):
            acc[pl.ds(toks[u], 1), :] = sums[u]
        return carry

    lax.fori_loop(0, cap // COMBINE_UNROLL, add_rows, 0)

    @pl.when(e == n_e - 1)
    def _():
        n_blk = seq // row_block

        def f32_copy(i):
            return pltpu.make_async_copy(acc.at[pl.ds(i * row_block, row_block), :],
                                         xo_hbm.at[pl.ds(b * seq + i * row_block, row_block), :],
                                         out_sem.at[0])

        def bf16_copy(i, slot):
            return pltpu.make_async_copy(xb_stage.at[slot],
                                         xb_hbm.at[pl.ds(b * seq + i * row_block, row_block), :],
                                         out_sem.at[1 + slot])

        def ln_block(i, carry):
            rows = pl.ds(pl.multiple_of(i * row_block, row_block), row_block)
            y = _layer_norm(acc[rows, :], g_ref[...], b_ref[...])
            acc[rows, :] = y
            f32_copy(i).start()
            slot = i % 2

            @pl.when(i >= 2)
            def _():
                bf16_copy(i - 2, slot).wait()

            xb_stage[slot] = y.astype(BF16)
            bf16_copy(i, slot).start()
            return carry

        lax.fori_loop(0, n_blk, ln_block, 0)
        for i in range(max(n_blk - 2, 0), n_blk):
            bf16_copy(i, i % 2).wait()

        def drain(i, carry):
            f32_copy(i).wait()
            return carry

        lax.fori_loop(0, n_blk, drain, 0)


def _combine(idx_be, y, x, g, b, batch, seq, cap, n_e):
    d = x.shape[1]
    row_block = min(256, seq)
    kern = functools.partial(_combine_kernel, cap=cap, seq=seq, n_e=n_e, row_block=row_block)
    const = lambda b, e: (0, 0)
    return pl.pallas_call(
        kern,
        grid=(batch, n_e),
        in_specs=[pl.BlockSpec((1, 1, cap), lambda b, e: (b * n_e + e, 0, 0), memory_space=pltpu.SMEM),
                  pl.BlockSpec((cap, d), lambda b, e: (e * batch + b, 0)),
                  pl.BlockSpec(memory_space=pl.ANY),
                  pl.BlockSpec((1, d), const), pl.BlockSpec((1, d), const)],
        out_specs=[pl.BlockSpec(memory_space=pl.ANY), pl.BlockSpec(memory_space=pl.ANY)],
        out_shape=[jax.ShapeDtypeStruct((batch * seq, d), F32),
                   jax.ShapeDtypeStruct((batch * seq, d), BF16)],
        scratch_shapes=[pltpu.VMEM((seq, d), F32), pltpu.VMEM((2, row_block, d), BF16),
                        pltpu.SemaphoreType.DMA(()), pltpu.SemaphoreType.DMA((3,))],
        compiler_params=_cparams(("arbitrary", "arbitrary")),
        name="moe_combine_layernorm",
    )(idx_be, y, x, g, b)


def _rope_tables(seq, rot_dim, split_halves=False):
    rows = seq // GRID_W
    row = jnp.repeat(jnp.arange(rows, dtype=jnp.int32), GRID_W).astype(F32)
    col = jnp.tile(jnp.arange(GRID_W, dtype=jnp.int32), rows).astype(F32)
    half = rot_dim // 2
    inv_freq = ROPE_THETA ** (-jnp.arange(0, half, 2, dtype=F32) / half)
    ang = jnp.concatenate([row[:, None] * inv_freq, col[:, None] * inv_freq], axis=-1)
    cos, sin = jnp.cos(ang), jnp.sin(ang)
    if split_halves:
        cos2 = jnp.concatenate([cos, cos], axis=1)
        sin2 = jnp.concatenate([-sin, sin], axis=1)
    else:
        cos2 = jnp.repeat(cos, 2, axis=1)
        sin2 = jnp.stack([-sin, sin], axis=-1).reshape(seq, rot_dim)
    pad = LANES - rot_dim
    if pad:
        cos2 = jnp.concatenate([cos2, jnp.ones((seq, pad), F32)], axis=1)
        sin2 = jnp.concatenate([sin2, jnp.zeros((seq, pad), F32)], axis=1)
    return cos2, sin2


def _moe(x1, xp1, logits_t, w1, w3, w2, ln_g, ln_b, layer, batch, seq):
    n_e = w1.shape[1]
    cap = CAPACITY_FACTOR * seq // n_e
    idx_col, gate_col = _route(logits_t, batch, seq, cap)
    idx_be = idx_col.reshape(batch * n_e, 1, cap)
    idx_eb = idx_col.reshape(batch, n_e, cap).transpose(1, 0, 2).reshape(n_e * batch, 1, cap)
    gate_eb = gate_col.reshape(batch, n_e, cap).transpose(1, 0, 2).reshape(n_e * batch * cap, 1)
    y = _ffn(idx_eb, xp1, w1, w3, w2, layer, gate_eb, batch, seq, cap, tf=256)
    return _combine(idx_be, y, x1, ln_g, ln_b, batch, seq, cap, n_e)


def _split_router(w_router):
    wr = w_router.T
    hi = wr.astype(BF16)
    lo = (wr - hi.astype(F32)).astype(BF16)
    return hi, lo


def _even_mixer(xb, batch, seq, w_in, q_norm, w_uq, kv_norm, w_ukv, gate_w2, gate_b, gla_norm,
                cos_r, sin_r):
    d = w_in.shape[0]
    o_cq, o_ckv, o_kr = 0, MLA_Q_RANK, MLA_Q_RANK + MLA_KV_RANK
    o_gq = o_kr + MLA_ROPE_DIM
    o_gk = o_gq + GLA_HEADS * GLA_DK
    o_gv = o_gk + GLA_HEADS * GLA_DK
    o_gr = o_gv + GLA_HEADS * GLA_DV
    o_lat = o_gr + GLA_HEADS * GLA_DV
    o_end = o_lat + 2 * GLA_GATE_RANK
    zeros = lambda n: jnp.zeros((d, n), BF16)
    tail_pad = LANES - MLA_ROPE_DIM - 2 * GLA_GATE_RANK
    w_in = w_in.astype(BF16)
    w_in_l = jnp.concatenate([
        w_in[:, o_cq:o_ckv], w_in[:, o_gq:o_gk], w_in[:, o_gk:o_gv], w_in[:, o_ckv:o_kr],
        w_in[:, o_kr:o_gq], w_in[:, o_lat:o_end], zeros(tail_pad), zeros(H_GV - H_TAIL - LANES),
        w_in[:, o_gv:o_gr], w_in[:, o_gr:o_lat]], axis=1)
    h = _matmul(xb, w_in_l, F32, tm=1024, tn=1024)

    qk = MLA_NOPE_DIM + MLA_ROPE_DIM
    wuq_l = jnp.pad(w_uq.reshape(MLA_Q_RANK, MLA_HEADS, qk),
                    ((0, 0), (0, 0), (0, MLA_QK_PAD - qk))).reshape(MLA_Q_RANK, MLA_HEADS * MLA_QK_PAD)
    wukv3 = w_ukv.reshape(MLA_KV_RANK, MLA_HEADS, MLA_NOPE_DIM + MLA_V_DIM)
    wukv_l = jnp.concatenate([wukv3[:, :, :MLA_NOPE_DIM].reshape(MLA_KV_RANK, -1),
                              wukv3[:, :, MLA_NOPE_DIM:].reshape(MLA_KV_RANK, -1)], axis=1)
    gw = GLA_HEADS * GLA_DK
    wg = jnp.zeros((LANES, 2 * gw), F32)
    wg = wg.at[MLA_ROPE_DIM:MLA_ROPE_DIM + GLA_GATE_RANK, :gw].set(gate_w2[0])
    wg = wg.at[MLA_ROPE_DIM + GLA_GATE_RANK:MLA_ROPE_DIM + 2 * GLA_GATE_RANK, gw:].set(gate_w2[1])
    gb = jnp.concatenate([gate_b[0], gate_b[1]])[None, :]
    q_p, k_p, v_p, la = _mla_prep(h, q_norm[None, :], wuq_l.astype(BF16), kv_norm[None, :],
                                  wukv_l.astype(BF16), wg.astype(BF16), gb, cos_r, sin_r, seq, tm=256)
    o_mla = _attention(q_p, k_p, v_p, batch=batch, seq=seq, kv_heads=MLA_HEADS, group=1,
                       dqk=MLA_QK_PAD, dv=MLA_V_DIM, q_off=0, k_off=0, v_off=0, tq=2048)
    o_gla = _gla(h, la, gla_norm.reshape(GLA_HEADS, 1, GLA_DV), batch, seq)
    return o_mla, o_gla


def _odd_mixer(xb, batch, seq, w_qkv, q_norm, k_norm, cos_g, sin_g):
    hd = GQA_HEAD_DIM
    scale = hd ** -0.5 * LOG2_E
    n_qk, n_v = (GQA_HEADS + GQA_KV_HEADS) * hd, GQA_KV_HEADS * hd
    split = jnp.concatenate([jnp.arange(0, hd, 2), jnp.arange(1, hd, 2)])
    d_in = w_qkv.shape[0]
    w_qk = w_qkv[:, :n_qk].reshape(d_in, n_qk // hd, hd // 2, 2).transpose(0, 1, 3, 2).reshape(d_in, n_qk)
    w_l = jnp.concatenate([w_qk, w_qkv[:, n_qk:]], axis=1)
    gain_row = jnp.concatenate([jnp.tile(q_norm[split] * scale, GQA_HEADS), jnp.tile(k_norm[split], GQA_KV_HEADS),
                                jnp.ones((n_v,), F32)]).reshape(-1, 1, 4 * hd)
    normed_row = jnp.concatenate([jnp.ones((n_qk,), F32), jnp.zeros((n_v,), F32)]).reshape(-1, 1, 4 * hd)
    gains = jnp.concatenate([gain_row, normed_row], axis=1)
    qkv = _qkv_proj(xb, w_l.astype(BF16), gains, cos_g, sin_g, seq, tm=1024)
    group = GQA_HEADS // GQA_KV_HEADS
    return _attention(qkv, qkv, qkv, batch=batch, seq=seq, kv_heads=GQA_KV_HEADS, group=group,
                      dqk=hd, dv=hd, q_off=0, k_off=GQA_HEADS, v_off=GQA_HEADS + GQA_KV_HEADS, tq=512)


def kernel(x, mix_w_in, mla_q_norm, mla_w_uq, mla_kv_norm, mla_w_ukv, gla_gate_w2, gla_gate_b,
           gla_out_norm, mix_w_out, gqa_w_qkv, gqa_q_norm, gqa_k_norm, gqa_w_out,
           moe_router, moe_w1, moe_w3, moe_w2, ln_mix_g, ln_mix_b, ln_ffn_g, ln_ffn_b):
    batch, seq, d = x.shape
    cos_r, sin_r = _rope_tables(seq, MLA_ROPE_DIM)
    cos_g, sin_g = _rope_tables(seq, GQA_HEAD_DIM, split_halves=True)
    xf = x.reshape(batch * seq, d)
    xb = xf.astype(BF16)
    n_layers = moe_router.shape[0]
    for layer in range(n_layers):
        i = layer // 2
        wr_hi, wr_lo = _split_router(moe_router[layer])
        g_mix, b_mix = ln_mix_g[layer][None, :], ln_mix_b[layer][None, :]
        if layer % 2 == 0:
            o_mla, o_gla = _even_mixer(xb, batch, seq, mix_w_in[i], mla_q_norm[i], mla_w_uq[i],
                                       mla_kv_norm[i], mla_w_ukv[i], gla_gate_w2[i], gla_gate_b[i],
                                       gla_out_norm[i], cos_r, sin_r)
            w_out = mix_w_out[i].astype(BF16)
            n_mla = MLA_HEADS * MLA_V_DIM
            x1, xp1, logits_t = _proj_ln([o_mla, o_gla], [w_out[:n_mla], w_out[n_mla:]], xf,
                                         g_mix, b_mix, wr_hi, wr_lo, tm=512)
        else:
            o = _odd_mixer(xb, batch, seq, gqa_w_qkv[i], gqa_q_norm[i], gqa_k_norm[i], cos_g, sin_g)
            x1, xp1, logits_t = _proj_ln([o], [gqa_w_out[i].astype(BF16)], xf,
                                         g_mix, b_mix, wr_hi, wr_lo, tm=512)
        xf, xb = _moe(x1, xp1, logits_t, moe_w1, moe_w3, moe_w2, ln_ffn_g[layer][None, :],
                      ln_ffn_b[layer][None, :], layer, batch, seq)
    return xf.reshape(batch, seq, d)
```

```python
import functools

import jax
import jax.numpy as jnp
from jax import lax
from jax.experimental import pallas as pl
from jax.experimental.pallas import tpu as pltpu

F32 = jnp.float32
BF16 = jnp.bfloat16

GRID_W = 64
ROPE_THETA = 10000.0
LN_EPS = 1e-5
RMS_EPS = 1e-6
DEPTH = 4
DEEPNORM_ALPHA = (2.0 * DEPTH) ** 0.25

MLA_HEADS = 8
MLA_Q_RANK = 512
MLA_KV_RANK = 256
MLA_NOPE_DIM = 128
MLA_ROPE_DIM = 64
MLA_V_DIM = 128
MLA_QK_PAD = 256

GLA_HEADS = 4
GLA_DK = 128
GLA_DV = 256
GLA_GATE_RANK = 16
GLA_GATE_TAU = 16.0
GLA_CHUNK = 64
GLA_GROUP = 256
GLA_SCAN_UNROLL = 8
GLA_GROUP_UNROLL = 4

GQA_HEADS = 16
GQA_KV_HEADS = 4
GQA_HEAD_DIM = 128

N_EXPERTS = 16
CAPACITY_FACTOR = 2

LANES = 128
VMEM_LIMIT_BYTES = 56 * 1024 * 1024

LOG2_E = 1.4426950408889634
ATTN_SAFE_LOG2_RANGE = 60.0

H_CQ = 0
H_GQ = 512
H_GK = 1024
H_CKV = 1536
H_TAIL = 1792
H_GV = 2048
H_GR = 3072
H_WIDTH = 4096


def _cparams(sem):
    return pltpu.CompilerParams(dimension_semantics=sem, vmem_limit_bytes=VMEM_LIMIT_BYTES)


def _nt_dot(a, b):
    return lax.dot_general(a, b, (((1,), (1,)), ((), ())), preferred_element_type=F32)


def _tn_dot(a, b):
    return lax.dot_general(a, b, (((0,), (0,)), ((), ())), preferred_element_type=F32)


def _rope(x, cos, sin_signed):
    lane = lax.broadcasted_iota(jnp.int32, x.shape, 1)
    partner = jnp.where((lane & 1) == 0, pltpu.roll(x, LANES - 1, 1), pltpu.roll(x, 1, 1))
    return x * cos + partner * sin_signed


def _rms(x, gain):
    return x * lax.rsqrt(jnp.mean(x * x, axis=-1, keepdims=True) + RMS_EPS) * gain


def _store_token_major(ref, val, first_row=0):
    count, n = val.shape[0], val.shape[1] // LANES
    for j in range(n):
        ref[pl.ds(first_row * n + j, count, stride=n), :] = val[:, j * LANES:(j + 1) * LANES]


def _layer_norm(z, g, b):
    mu = jnp.mean(z, axis=-1, keepdims=True)
    zc = z - mu
    var = jnp.mean(zc * zc, axis=-1, keepdims=True)
    return zc * lax.rsqrt(var + LN_EPS) * g + b


def _mm_kernel(x_ref, w_ref, o_ref):
    o_ref[...] = jnp.dot(x_ref[...], w_ref[...], preferred_element_type=F32).astype(o_ref.dtype)


def _matmul(x, w, out_dtype, tm, tn):
    m, k = x.shape
    n = w.shape[1]
    tm, tn = min(tm, m), min(tn, n)
    return pl.pallas_call(
        _mm_kernel,
        grid=(n // tn, m // tm),
        in_specs=[pl.BlockSpec((tm, k), lambda j, i: (i, 0)),
                  pl.BlockSpec((k, tn), lambda j, i: (0, j))],
        out_specs=pl.BlockSpec((tm, tn), lambda j, i: (i, j)),
        out_shape=jax.ShapeDtypeStruct((m, n), out_dtype),
        compiler_params=_cparams(("parallel", "parallel")),
        name="dense_matmul",
    )(x, w)


def _qkv_kernel(x_ref, w_ref, g_ref, ones_ref, cos_ref, sin_ref, o_ref, *, heads_per_tile):
    acc = jnp.dot(x_ref[...], w_ref[...], preferred_element_type=F32)
    sq = acc * acc
    sq_hi = sq.astype(BF16)
    sq_lo = (sq - sq_hi.astype(F32)).astype(BF16)
    ones_bd = ones_ref[...]
    ms = (jnp.dot(sq_hi, ones_bd, preferred_element_type=F32)
          + jnp.dot(sq_lo, ones_bd, preferred_element_type=F32)) * (1.0 / GQA_HEAD_DIM)
    y = acc * lax.rsqrt(ms + RMS_EPS) * g_ref[0:1, :]
    cos = cos_ref[...]
    sin = sin_ref[...]
    for hd in range(heads_per_tile):
        sl = slice(hd * GQA_HEAD_DIM, (hd + 1) * GQA_HEAD_DIM)
        yh = y[:, sl]
        rot = yh * cos + pltpu.roll(yh, GQA_HEAD_DIM // 2, 1) * sin
        o_ref[:, sl] = jnp.where(g_ref[1:2, sl] > 0.5, rot, acc[:, sl]).astype(o_ref.dtype)


def _qkv_proj(xb, w, gains, cos, sin, seq, tm):
    m, k = xb.shape
    n = w.shape[1]
    tn = 4 * GQA_HEAD_DIM
    tm = min(tm, seq)
    n_tiles = n // tn
    pos_blocks = seq // tm
    kern = functools.partial(_qkv_kernel, heads_per_tile=4)
    col_head = jnp.arange(tn, dtype=jnp.int32) // GQA_HEAD_DIM
    ones_bd = (col_head[:, None] == col_head[None, :]).astype(BF16)
    return pl.pallas_call(
        kern,
        grid=(n_tiles, m // tm),
        in_specs=[pl.BlockSpec((tm, k), lambda j, i: (i, 0)),
                  pl.BlockSpec((k, tn), lambda j, i: (0, j)),
                  pl.BlockSpec((None, 2, tn), lambda j, i: (j, 0, 0)),
                  pl.BlockSpec((tn, tn), lambda j, i: (0, 0)),
                  pl.BlockSpec((tm, LANES), lambda j, i: (i % pos_blocks, 0)),
                  pl.BlockSpec((tm, LANES), lambda j, i: (i % pos_blocks, 0))],
        out_specs=pl.BlockSpec((tm, tn), lambda j, i: (i, j)),
        out_shape=jax.ShapeDtypeStruct((m, n), BF16),
        compiler_params=_cparams(("parallel", "parallel")),
        name="gqa_qkv_proj",
    )(xb, w, gains, ones_bd, cos, sin)


def _attn_kernel(q_ref, k_ref, v_ref, o_ref, vx_s, kmax_s, *, group, dqk, dv, kc_single, kc_online):
    tq = q_ref.shape[0]
    seq = k_ref.shape[0]
    rows = group * tq

    @pl.when(pl.program_id(2) == 0)
    def _():
        lane = lax.broadcasted_iota(jnp.int32, (seq, LANES), 1)
        vx_s[:, :dv] = v_ref[...]
        vx_s[:, dv:] = jnp.where(lane == 0, 1.0, 0.0).astype(BF16)
        k32 = k_ref[...].astype(F32)
        k_sq = jnp.sum(k32 * k32, axis=1, keepdims=True)
        kmax_s[...] = jnp.max(k_sq, axis=0, keepdims=True)

    q = jnp.concatenate([q_ref[:, g * dqk:(g + 1) * dqk] for g in range(group)], axis=0)
    q32 = q.astype(F32)
    bound = jnp.sqrt(jnp.sum(q32 * q32, axis=1, keepdims=True) * kmax_s[...])
    single_pass = jnp.max(bound) <= ATTN_SAFE_LOG2_RANGE

    def write(acc):
        o = acc[:, :dv] / acc[:, dv:dv + 1]
        for g in range(group):
            o_ref[:, g * dv:(g + 1) * dv] = o[g * tq:(g + 1) * tq, :].astype(o_ref.dtype)

    @pl.when(single_pass)
    def _():
        kc = kc_single
        acc = jnp.zeros((rows, dv + LANES), F32)
        for c in range(seq // kc):
            p = jnp.exp2(_nt_dot(q, k_ref[c * kc:(c + 1) * kc, :]) - bound)
            acc = acc + jnp.dot(p.astype(BF16), vx_s[c * kc:(c + 1) * kc, :], preferred_element_type=F32)
        write(acc)

    @pl.when(jnp.logical_not(single_pass))
    def _():
        kc = kc_online
        m = jnp.full((rows, 1), -jnp.inf, F32)
        acc = jnp.zeros((rows, dv + LANES), F32)
        for c in range(seq // kc):
            s = _nt_dot(q, k_ref[c * kc:(c + 1) * kc, :])
            m_new = jnp.maximum(m, jnp.max(s, axis=1, keepdims=True))
            p = jnp.exp2(s - m_new)
            acc = acc * jnp.exp2(m - m_new) + jnp.dot(p.astype(BF16), vx_s[c * kc:(c + 1) * kc, :],
                                                      preferred_element_type=F32)
            m = m_new
        write(acc)


def _attention(q_arr, k_arr, v_arr, *, batch, seq, kv_heads, group, dqk, dv,
               q_off, k_off, v_off, tq):
    tq = min(tq, seq)
    nq = seq // tq
    kern = functools.partial(_attn_kernel, group=group, dqk=dqk, dv=dv,
                             kc_single=min(256, seq), kc_online=min(1024, seq))
    return pl.pallas_call(
        kern,
        grid=(batch, kv_heads, nq),
        in_specs=[pl.BlockSpec((tq, group * dqk), lambda b, h, i: (b * nq + i, q_off + h)),
                  pl.BlockSpec((seq, dqk), lambda b, h, i: (b, k_off + h)),
                  pl.BlockSpec((seq, dv), lambda b, h, i: (b, v_off + h))],
        out_specs=pl.BlockSpec((tq, group * dv), lambda b, h, i: (b * nq + i, h)),
        out_shape=jax.ShapeDtypeStruct((batch * seq, kv_heads * group * dv), BF16),
        scratch_shapes=[pltpu.VMEM((seq, dv + LANES), BF16), pltpu.VMEM((1, 1), F32)],
        compiler_params=_cparams(("arbitrary", "arbitrary", "arbitrary")),
        name="softmax_attention",
    )(q_arr, k_arr, v_arr)


def _mla_prep_kernel(cq_ref, ckv_ref, tail_ref, qn_ref, wuq_ref, kvn_ref, wukv_ref, wg_ref, gb_ref,
                     cos_ref, sin_ref, q_ref, k_ref, v_ref, la_ref):
    cos = cos_ref[...]
    sin = sin_ref[...]
    scale = (MLA_NOPE_DIM + MLA_ROPE_DIM) ** -0.5 * LOG2_E
    q = jnp.dot(_rms(cq_ref[...], qn_ref[...]).astype(BF16), wuq_ref[...],
                preferred_element_type=F32) * scale
    kv = jnp.dot(_rms(ckv_ref[...], kvn_ref[...]).astype(BF16), wukv_ref[...],
                 preferred_element_type=F32)
    tail = tail_ref[...]
    lane = lax.broadcasted_iota(jnp.int32, tail.shape, 1)
    k_pe = jnp.where(lane < MLA_ROPE_DIM, _rope(tail, cos, sin), 0.0).astype(BF16)
    for hd in range(MLA_HEADS):
        c0 = hd * MLA_QK_PAD
        q_ref[:, c0:c0 + LANES] = q[:, c0:c0 + LANES].astype(BF16)
        q_ref[:, c0 + LANES:c0 + 2 * LANES] = _rope(q[:, c0 + LANES:c0 + 2 * LANES], cos, sin).astype(BF16)
        k_ref[:, c0:c0 + LANES] = kv[:, hd * LANES:(hd + 1) * LANES].astype(BF16)
        k_ref[:, c0 + LANES:c0 + 2 * LANES] = k_pe
    nv = MLA_HEADS * MLA_V_DIM
    v_ref[...] = kv[:, nv:].astype(BF16)
    gate = jnp.dot(tail.astype(BF16), wg_ref[...], preferred_element_type=F32) + gb_ref[...]
    log_sig = jnp.minimum(gate, 0.0) - jnp.log1p(jnp.exp(-jnp.abs(gate)))
    la_ref[...] = log_sig / GLA_GATE_TAU


def _mla_prep(h, qn, wuq, kvn, wukv, wg, gb, cos, sin, seq, tm):
    m = h.shape[0]
    tm = min(tm, seq)
    pos_blocks = seq // tm
    qw = MLA_HEADS * MLA_QK_PAD
    vw = MLA_HEADS * MLA_V_DIM
    gw = 2 * GLA_HEADS * GLA_DK
    const = lambda i: (0, 0)
    return pl.pallas_call(
        _mla_prep_kernel,
        grid=(m // tm,),
        in_specs=[pl.BlockSpec((tm, MLA_Q_RANK), lambda i: (i, H_CQ // MLA_Q_RANK)),
                  pl.BlockSpec((tm, MLA_KV_RANK), lambda i: (i, H_CKV // MLA_KV_RANK)),
                  pl.BlockSpec((tm, LANES), lambda i: (i, H_TAIL // LANES)),
                  pl.BlockSpec((1, MLA_Q_RANK), const),
                  pl.BlockSpec((MLA_Q_RANK, qw), const),
                  pl.BlockSpec((1, MLA_KV_RANK), const),
                  pl.BlockSpec((MLA_KV_RANK, 2 * vw), const),
                  pl.BlockSpec((LANES, gw), const),
                  pl.BlockSpec((1, gw), const),
                  pl.BlockSpec((tm, LANES), lambda i: (i % pos_blocks, 0)),
                  pl.BlockSpec((tm, LANES), lambda i: (i % pos_blocks, 0))],
        out_specs=[pl.BlockSpec((tm, qw), lambda i: (i, 0)),
                   pl.BlockSpec((tm, qw), lambda i: (i, 0)),
                   pl.BlockSpec((tm, vw), lambda i: (i, 0)),
                   pl.BlockSpec((tm, gw), lambda i: (i, 0))],
        out_shape=[jax.ShapeDtypeStruct((m, qw), BF16),
                   jax.ShapeDtypeStruct((m, qw), BF16),
                   jax.ShapeDtypeStruct((m, vw), BF16),
                   jax.ShapeDtypeStruct((m, gw), F32)],
        compiler_params=_cparams(("parallel",)),
        name="mla_prep",
    )(h, h, h, qn, wuq, kvn, wukv, wg, gb, cos, sin)


def _split3(x):
    hi = x.astype(BF16)
    r1 = x - hi.astype(F32)
    mid = r1.astype(BF16)
    lo = (r1 - mid.astype(F32)).astype(BF16)
    return hi, mid, lo


def _gla_kernel(q_ref, k_ref, v_ref, gr_ref, laf_ref, lab_ref, gn_ref, o_ref,
                of_s, ob_s, qf_s, qb_s, kf_s, kb_s, df_s, db_s, stf_s, stb_s):
    seq = q_ref.shape[0]
    c = GLA_CHUNK
    r = min(GLA_GROUP, seq)
    cpg = r // c
    n_groups = seq // r
    group_unroll = min(GLA_GROUP_UNROLL, n_groups)
    scan_unroll = min(GLA_SCAN_UNROLL, seq // c)
    n_chunks = seq // c
    scale = GLA_DK ** -0.5

    row = lax.broadcasted_iota(jnp.int32, (r, r), 0)
    col = lax.broadcasted_iota(jnp.int32, (r, r), 1)
    same = (row // c) == (col // c)
    tril = same & (col <= row)
    triu = same & (col >= row)
    tril_b = tril.astype(F32).astype(BF16)
    triu_b = triu.astype(F32).astype(BF16)

    def group_pair(gp, carry):
        for u in range(group_unroll):
            group_body(gp * group_unroll + u)
        return carry

    def group_body(gi):
        r0 = pl.multiple_of(gi * r, r)
        rows = pl.ds(r0, r)
        q = q_ref[rows, :] * scale
        k = k_ref[rows, :]
        v = v_ref[rows, :].astype(BF16)
        for la_ref, mask, mask_b, edge, o_s, q_s, k_s, d_s in (
                (laf_ref, tril, tril_b, c - 1, of_s, qf_s, kf_s, df_s),
                (lab_ref, triu, triu_b, 0, ob_s, qb_s, kb_s, db_s)):
            la = la_ref[rows, :]
            la_hi = la.astype(BF16)
            la_lo = (la - la_hi.astype(F32)).astype(BF16)
            both = jnp.dot(mask_b, jnp.concatenate([la_hi, la_lo], axis=1), preferred_element_type=F32)
            b = both[:, :GLA_DK] + both[:, GLA_DK:]
            b_edge = jnp.concatenate(
                [jnp.broadcast_to(b[ci * c + edge:ci * c + edge + 1, :], (c, GLA_DK)) for ci in range(cpg)],
                axis=0)
            q_in = (q * jnp.exp(b)).astype(BF16)
            k_in = (k * jnp.exp(-b)).astype(BF16)
            k_st = (k * jnp.exp(b_edge - b)).astype(BF16)
            att = jnp.where(mask, _nt_dot(q_in, k_in), 0.0)
            o_s[rows, :] = jnp.dot(att.astype(BF16), v, preferred_element_type=F32)
            q_s[rows, :] = q_in
            k_s[rows, :] = k_st
            for ci in range(cpg):
                d_s[pl.ds(gi * cpg + ci, 1), :] = jnp.exp(b[ci * c + edge:ci * c + edge + 1, :])

    lax.fori_loop(0, n_groups // group_unroll, group_pair, 0)

    stf_s[...] = jnp.zeros_like(stf_s)
    stb_s[...] = jnp.zeros_like(stb_s)

    def chunk_body(i, carry):
        for u in range(scan_unroll):
            fwd_n = i * scan_unroll + u
            for n, o_s, q_s, k_s, d_s, st_s in ((fwd_n, of_s, qf_s, kf_s, df_s, stf_s),
                                                (n_chunks - 1 - fwd_n, ob_s, qb_s, kb_s, db_s, stb_s)):
                rows = pl.ds(pl.multiple_of(n * c, c), c)
                st = st_s[...]
                o_s[rows, :] += _nt_dot(q_s[rows, :], st.astype(BF16))
                ds = _tn_dot(v_ref[rows, :].astype(BF16), k_s[rows, :])
                st_s[...] = d_s[pl.ds(n, 1), :] * st + ds
        return carry

    lax.fori_loop(0, n_chunks // scan_unroll, chunk_body, 0)

    gain = gn_ref[...]

    def out_body(gi, carry):
        rows = pl.ds(pl.multiple_of(gi * r, r), r)
        o = _rms(of_s[rows, :] + ob_s[rows, :], gain)
        gr = gr_ref[rows, :]
        o_ref[rows, :] = (o * (gr * jax.nn.sigmoid(gr))).astype(o_ref.dtype)
        return carry

    lax.fori_loop(0, n_groups, out_body, 0)


def _gla(h, la, gn, batch, seq):
    dk, dv, nh = GLA_DK, GLA_DV, GLA_HEADS
    n_chunks = seq // GLA_CHUNK
    return pl.pallas_call(
        _gla_kernel,
        grid=(batch, nh),
        in_specs=[pl.BlockSpec((seq, dk), lambda b, hd: (b, H_GQ // dk + hd)),
                  pl.BlockSpec((seq, dk), lambda b, hd: (b, H_GK // dk + hd)),
                  pl.BlockSpec((seq, dv), lambda b, hd: (b, H_GV // dv + hd)),
                  pl.BlockSpec((seq, dv), lambda b, hd: (b, H_GR // dv + hd)),
                  pl.BlockSpec((seq, dk), lambda b, hd: (b, hd)),
                  pl.BlockSpec((seq, dk), lambda b, hd: (b, nh + hd)),
                  pl.BlockSpec((None, 1, dv), lambda b, hd: (hd, 0, 0))],
        out_specs=pl.BlockSpec((seq, dv), lambda b, hd: (b, hd)),
        out_shape=jax.ShapeDtypeStruct((batch * seq, nh * dv), BF16),
        scratch_shapes=[pltpu.VMEM((seq, dv), F32), pltpu.VMEM((seq, dv), F32),
                        pltpu.VMEM((seq, dk), BF16), pltpu.VMEM((seq, dk), BF16),
                        pltpu.VMEM((seq, dk), BF16), pltpu.VMEM((seq, dk), BF16),
                        pltpu.VMEM((n_chunks, dk), F32), pltpu.VMEM((n_chunks, dk), F32),
                        pltpu.VMEM((dv, dk), F32), pltpu.VMEM((dv, dk), F32)],
        compiler_params=_cparams(("arbitrary", "arbitrary")),
        name="gla_bidirectional",
    )(h, h, h, h, la, la, gn)


PROJ_SUBTILES = 1


def _proj_ln_kernel(*refs, n_in):
    a_refs = refs[:n_in]
    w_refs = refs[n_in:2 * n_in]
    x_ref, g_ref, b_ref, wr_ref, xo_ref, xp_ref, lg_ref = refs[2 * n_in:]
    tm, d = x_ref.shape
    half = d // 2
    sub = tm // PROJ_SUBTILES
    wr = wr_ref[...]
    n_e = wr.shape[0] // 2
    for t in range(PROJ_SUBTILES):
        rows = slice(t * sub, (t + 1) * sub)
        mix = jnp.dot(a_refs[0][rows, :], w_refs[0][...], preferred_element_type=F32)
        for a_ref, w_ref in zip(a_refs[1:], w_refs[1:]):
            mix = mix + jnp.dot(a_ref[rows, :], w_ref[...], preferred_element_type=F32)
        y = _layer_norm(DEEPNORM_ALPHA * x_ref[rows, :] + mix, g_ref[...], b_ref[...])
        xo_ref[rows, :] = y
        y_hi = y.astype(BF16)
        y_hi32 = y_hi.astype(F32)
        bits = lax.bitcast_convert_type(y_hi32, jnp.int32)
        _store_token_major(xp_ref, lax.shift_right_logical(bits[:, :half], 16) | bits[:, half:],
                           first_row=t * sub)
        y_lo = (y - y_hi32).astype(BF16)
        by_hi = _nt_dot(wr, y_hi)
        lg_ref[:, rows] = (by_hi[:n_e, :] + by_hi[n_e:, :]) + _nt_dot(wr[:n_e, :], y_lo)


def _proj_ln(acts, weights, x, g, b, wr_split, tm):
    m, d = x.shape
    tm = min(tm, m)
    n_in = len(acts)
    n_e = wr_split.shape[0] // 2
    np_ = d // (2 * LANES)
    const = lambda i: (0, 0)
    in_specs = ([pl.BlockSpec((tm, a.shape[1]), lambda i: (i, 0)) for a in acts]
                + [pl.BlockSpec(w.shape, const) for w in weights]
                + [pl.BlockSpec((tm, d), lambda i: (i, 0)),
                   pl.BlockSpec((1, d), const), pl.BlockSpec((1, d), const),
                   pl.BlockSpec((2 * n_e, d), const)])
    return pl.pallas_call(
        functools.partial(_proj_ln_kernel, n_in=n_in),
        grid=(m // tm,),
        in_specs=in_specs,
        out_specs=[pl.BlockSpec((tm, d), lambda i: (i, 0)),
                   pl.BlockSpec((tm * np_, LANES), lambda i: (i, 0)),
                   pl.BlockSpec((n_e, tm), lambda i: (0, i))],
        out_shape=[jax.ShapeDtypeStruct((m, d), F32),
                   jax.ShapeDtypeStruct((m * np_, LANES), jnp.int32),
                   jax.ShapeDtypeStruct((n_e, m), F32)],
        compiler_params=_cparams(("parallel",)),
        name="out_proj_layernorm_router",
    )(*acts, *weights, x, g, b, wr_split)


def _route_kernel(lg_ref, idx_ref, gate_ref, aff_s, cum_s, start_s, *, cap):
    lg = lg_ref[...]
    ex = jnp.exp(lg - jnp.max(lg, axis=0, keepdims=True))
    aff = ex / jnp.sum(ex, axis=0, keepdims=True)
    bits = lax.bitcast_convert_type(aff, jnp.int32)
    n_e, seq = lg.shape
    nblk = seq // LANES
    capf = float(cap)

    def bisect(_, carry):
        lo, hi = carry
        mid = lo + ((hi - lo + 1) >> 1)
        cnt = jnp.sum(jnp.where(bits >= mid, 1.0, 0.0), axis=1, keepdims=True)
        ok = cnt >= capf
        return jnp.where(ok, mid, lo), jnp.where(ok, hi, mid - 1)

    lo0 = jnp.zeros((n_e, 1), jnp.int32)
    hi0 = jnp.full((n_e, 1), 0x7F800000, jnp.int32)
    thr, _ = lax.fori_loop(0, 32, bisect, (lo0, hi0))

    gt = bits > thr
    eq = bits == thr
    need = capf - jnp.sum(jnp.where(gt, 1.0, 0.0), axis=1, keepdims=True)

    r_i = lax.broadcasted_iota(jnp.int32, (LANES, LANES), 0)
    c_i = lax.broadcasted_iota(jnp.int32, (LANES, LANES), 1)
    upper = (r_i <= c_i).astype(F32).astype(BF16)
    eq_off = jnp.zeros((n_e, 1), F32)
    sel_off = jnp.zeros((n_e, 1), F32)
    for j in range(nblk):
        blk = slice(j * LANES, (j + 1) * LANES)
        eq01 = jnp.where(eq[:, blk], 1.0, 0.0)
        eq_cum = jnp.dot(eq01.astype(BF16), upper, preferred_element_type=F32) + eq_off
        eq_off = eq_cum[:, LANES - 1:LANES]
        tie_rank = eq_cum - eq01
        sel01 = jnp.where(gt[:, blk] | (eq[:, blk] & (tie_rank < need)), 1.0, 0.0)
        sel_cum = jnp.dot(sel01.astype(BF16), upper, preferred_element_type=F32) + sel_off
        rows = slice(j * n_e, (j + 1) * n_e)
        cum_s[rows, :] = sel_cum
        start_s[rows, :] = jnp.broadcast_to(sel_off, (n_e, LANES))
        aff_s[rows, :] = aff[:, blk]
        sel_off = sel_cum[:, LANES - 1:LANES]

    slot_row = lax.broadcasted_iota(jnp.int32, (1, cap), 1).astype(F32)
    slot_col = lax.broadcasted_iota(jnp.int32, (cap, 1), 0).astype(F32)
    lane = lax.broadcasted_iota(jnp.int32, (cap, LANES), 1).astype(F32)
    blk_id = lax.broadcasted_iota(jnp.int32, (nblk, LANES), 0).astype(F32).astype(BF16)

    def extract(e, carry):
        rows = pl.ds(e, nblk, stride=n_e)
        cum_e = cum_s[rows, :]
        start_e = start_s[rows, :][:, 0:1]
        end_e = cum_e[:, LANES - 1:LANES]
        in_blk = jnp.where((start_e <= slot_row) & (slot_row < end_e), 1.0, 0.0).astype(BF16)
        cum_hi = jnp.floor(cum_e * (1.0 / 32.0))
        cum_lo = cum_e - 32.0 * cum_hi
        g = 32.0 * _tn_dot(in_blk, cum_hi.astype(BF16)) + _tn_dot(in_blk, cum_lo.astype(BF16))
        blk_of_slot = _tn_dot(in_blk, blk_id)[:, 0:1]
        pos = jnp.sum(jnp.where(g <= slot_col, 1.0, 0.0), axis=1, keepdims=True)
        a_hi, a_mid, a_lo = _split3(aff_s[rows, :])
        a = (_tn_dot(in_blk, a_hi) + _tn_dot(in_blk, a_mid)) + _tn_dot(in_blk, a_lo)
        gate = jnp.sum(jnp.where(lane == pos, a, 0.0), axis=1, keepdims=True)
        out_rows = pl.ds(pl.multiple_of(e * cap, cap), cap)
        idx_ref[out_rows, :] = (float(LANES) * blk_of_slot + pos).astype(jnp.int32)
        gate_ref[out_rows, :] = gate
        return carry

    lax.fori_loop(0, n_e, extract, 0)


def _route(logits_t, batch, seq, cap):
    n_e = logits_t.shape[0]
    kern = functools.partial(_route_kernel, cap=cap)
    scratch_rows = (seq // LANES) * n_e
    return pl.pallas_call(
        kern,
        grid=(batch,),
        in_specs=[pl.BlockSpec((n_e, seq), lambda b: (0, b))],
        out_specs=[pl.BlockSpec((n_e * cap, 1), lambda b: (b, 0)),
                   pl.BlockSpec((n_e * cap, 1), lambda b: (b, 0))],
        out_shape=[jax.ShapeDtypeStruct((batch * n_e * cap, 1), jnp.int32),
                   jax.ShapeDtypeStruct((batch * n_e * cap, 1), F32)],
        scratch_shapes=[pltpu.VMEM((scratch_rows, LANES), F32), pltpu.VMEM((scratch_rows, LANES), F32),
                        pltpu.VMEM((scratch_rows, LANES), F32)],
        compiler_params=_cparams(("parallel",)),
        name="expert_choice_route",
    )(logits_t)


GATHER_UNROLL = 32


def _ffn_kernel(idx_cur, idx_nxt, xp_hbm, w1_ref, w3_ref, w2_ref, gate_ref, o_ref,
                land, x_lo, x_hi, sem, *, cap, seq, bpt, issue_steps, unpack_rows):
    m_tiles = pl.num_programs(1)
    f = pl.program_id(2)
    t = pl.program_id(0) * m_tiles + pl.program_id(1)
    n_tiles = pl.num_programs(0) * m_tiles
    slot = t % 2
    tm = bpt * cap
    half = x_lo.shape[1]
    npk = half // LANES

    def row_copy(dst_slot, r, src_row):
        return pltpu.make_async_copy(xp_hbm.at[pl.ds(pl.multiple_of(src_row * npk, npk), npk), :],
                                     land.at[dst_slot, pl.ds(pl.multiple_of(r * npk, npk), npk), :],
                                     sem.at[dst_slot])

    def issue(idx_ref, tile, dst_slot, bb, s0, n):
        src_base = ((tile % m_tiles) * bpt + bb) * seq
        dst_base = bb * cap

        def body(i, carry):
            for u in range(GATHER_UNROLL):
                s = s0 + i * GATHER_UNROLL + u
                row_copy(dst_slot, dst_base + s, src_base + idx_ref[bb, 0, s]).start()
            return carry

        lax.fori_loop(0, n // GATHER_UNROLL, body, 0)

    @pl.when((t == 0) & (f == 0))
    def _():
        for bb in range(bpt):
            issue(idx_cur, t, slot, bb, 0, cap)

    @pl.when(f == 0)
    def _():
        pltpu.make_async_copy(xp_hbm.at[pl.ds(0, tm * npk), :], land.at[slot], sem.at[slot]).wait()

        def unpack(i, carry):
            r0 = pl.multiple_of(i * unpack_rows, unpack_rows)
            rows = pl.ds(r0, unpack_rows)
            for j in range(npk):
                w = land[slot, pl.ds(r0 * npk + j, unpack_rows, stride=npk), :]
                cols = slice(j * LANES, (j + 1) * LANES)
                x_lo[rows, cols] = lax.bitcast_convert_type(w << 16, F32).astype(BF16)
                x_hi[rows, cols] = lax.bitcast_convert_type(w & jnp.int32(-65536), F32).astype(BF16)
            return carry

        lax.fori_loop(0, tm // unpack_rows, unpack, 0)
        o_ref[...] = jnp.zeros_like(o_ref)

    per_batch = issue_steps // bpt
    n = cap // per_batch
    for bb in range(bpt):
        @pl.when((f >= bb * per_batch) & (f < (bb + 1) * per_batch) & (t + 1 < n_tiles))
        def _(bb=bb):
            issue(idx_nxt, t + 1, 1 - slot, bb, (f - bb * per_batch) * n, n)

    w1 = w1_ref[...].astype(BF16)
    w3 = w3_ref[...].astype(BF16)
    lo, hi = x_lo[...], x_hi[...]
    h1 = (jnp.dot(lo, w1[:half], preferred_element_type=F32)
          + jnp.dot(hi, w1[half:], preferred_element_type=F32))
    h3 = (jnp.dot(lo, w3[:half], preferred_element_type=F32)
          + jnp.dot(hi, w3[half:], preferred_element_type=F32))
    hdn = (h1 * jax.nn.sigmoid(h1)) * h3
    part = jnp.dot(hdn.astype(BF16), w2_ref[...].astype(BF16), preferred_element_type=F32)
    is_last = f == pl.num_programs(2) - 1
    gate = jnp.where(is_last, gate_ref[...], 1.0)
    o_ref[...] = (o_ref[...] + part) * gate


def _ffn(idx_eb, xp, w1, w3, w2, layer, gate_col, batch, seq, cap, tf):
    _, n_e, d, ff = w1.shape
    bpt = min(2, batch)
    tm = bpt * cap
    mt = batch // bpt
    tf = min(tf, ff // bpt)
    nf = ff // tf
    assert nf >= 2 and ff % tf == 0
    issue_steps = bpt * min(2, nf // bpt)
    assert cap % (issue_steps // bpt * GATHER_UNROLL) == 0
    n_tiles = n_e * mt
    kern = functools.partial(_ffn_kernel, cap=cap, seq=seq, bpt=bpt, issue_steps=issue_steps,
                             unpack_rows=min(256, tm))
    smem_idx = lambda fn: pl.BlockSpec((bpt, 1, cap), fn, memory_space=pltpu.SMEM)
    return pl.pallas_call(
        kern,
        grid=(n_e, mt, nf),
        in_specs=[smem_idx(lambda e, m, f: (e * mt + m, 0, 0)),
                  smem_idx(lambda e, m, f: (jnp.minimum(e * mt + m + 1, n_tiles - 1), 0, 0)),
                  pl.BlockSpec(memory_space=pl.ANY),
                  pl.BlockSpec((None, None, d, tf), lambda e, m, f: (layer, e, 0, f)),
                  pl.BlockSpec((None, None, d, tf), lambda e, m, f: (layer, e, 0, f)),
                  pl.BlockSpec((None, None, tf, d), lambda e, m, f: (layer, e, f, 0)),
                  pl.BlockSpec((tm, 1), lambda e, m, f: (e * mt + m, 0))],
        out_specs=pl.BlockSpec((tm, d), lambda e, m, f: (e * mt + m, 0)),
        out_shape=jax.ShapeDtypeStruct((n_e * batch * cap, d), F32),
        scratch_shapes=[pltpu.VMEM((2, tm * (d // (2 * LANES)), LANES), jnp.int32),
                        pltpu.VMEM((tm, d // 2), BF16), pltpu.VMEM((tm, d // 2), BF16),
                        pltpu.SemaphoreType.DMA((2,))],
        compiler_params=_cparams(("arbitrary", "arbitrary", "arbitrary")),
        name="moe_expert_ffn",
    )(idx_eb, idx_eb, xp, w1, w3, w2, gate_col)


COMBINE_UNROLL = 8


def _combine_kernel(idx_ref, y_ref, x_hbm, g_ref, b_ref, xo_hbm, xb_hbm, acc, xb_stage, sem, out_sem,
                    *, cap, seq, n_e, row_block):
    b = pl.program_id(0)
    e = pl.program_id(1)

    @pl.when(e == 0)
    def _():
        cp = pltpu.make_async_copy(x_hbm.at[pl.ds(b * seq, seq), :], acc, sem)
        cp.start()
        cp.wait()

        def scale(i, carry):
            rows = pl.ds(pl.multiple_of(i * row_block, row_block), row_block)
            acc[rows, :] = acc[rows, :] * DEEPNORM_ALPHA
            return carry

        lax.fori_loop(0, seq // row_block, scale, 0)

    def add_rows(i, carry):
        s0 = i * COMBINE_UNROLL
        toks = [idx_ref[0, 0, s0 + u] for u in range(COMBINE_UNROLL)]
        sums = [acc[pl.ds(toks[u], 1), :] + y_ref[pl.ds(s0 + u, 1), :] for u in range(COMBINE_UNROLL)]
        for u in range(COMBINE_UNROLL):
            acc[pl.ds(toks[u], 1), :] = sums[u]
        return carry

    lax.fori_loop(0, cap // COMBINE_UNROLL, add_rows, 0)

    @pl.when(e == n_e - 1)
    def _():
        n_blk = seq // row_block

        def f32_copy(i):
            return pltpu.make_async_copy(acc.at[pl.ds(i * row_block, row_block), :],
                                         xo_hbm.at[pl.ds(b * seq + i * row_block, row_block), :],
                                         out_sem.at[0])

        def bf16_copy(i, slot):
            return pltpu.make_async_copy(xb_stage.at[slot],
                                         xb_hbm.at[pl.ds(b * seq + i * row_block, row_block), :],
                                         out_sem.at[1 + slot])

        def ln_block(i, carry):
            rows = pl.ds(pl.multiple_of(i * row_block, row_block), row_block)
            y = _layer_norm(acc[rows, :], g_ref[...], b_ref[...])
            acc[rows, :] = y
            f32_copy(i).start()
            slot = i % 2

            @pl.when(i >= 2)
            def _():
                bf16_copy(i - 2, slot).wait()

            xb_stage[slot] = y.astype(BF16)
            bf16_copy(i, slot).start()
            return carry

        lax.fori_loop(0, n_blk, ln_block, 0)
        for i in range(max(n_blk - 2, 0), n_blk):
            bf16_copy(i, i % 2).wait()

        def drain(i, carry):
            f32_copy(i).wait()
            return carry

        lax.fori_loop(0, n_blk, drain, 0)


def _combine(idx_be, y, x, g, b, batch, seq, cap, n_e):
    d = x.shape[1]
    row_block = min(256, seq)
    kern = functools.partial(_combine_kernel, cap=cap, seq=seq, n_e=n_e, row_block=row_block)
    const = lambda b, e: (0, 0)
    return pl.pallas_call(
        kern,
        grid=(batch, n_e),
        in_specs=[pl.BlockSpec((1, 1, cap), lambda b, e: (b * n_e + e, 0, 0), memory_space=pltpu.SMEM),
                  pl.BlockSpec((cap, d), lambda b, e: (e * batch + b, 0)),
                  pl.BlockSpec(memory_space=pl.ANY),
                  pl.BlockSpec((1, d), const), pl.BlockSpec((1, d), const)],
        out_specs=[pl.BlockSpec(memory_space=pl.ANY), pl.BlockSpec(memory_space=pl.ANY)],
        out_shape=[jax.ShapeDtypeStruct((batch * seq, d), F32),
                   jax.ShapeDtypeStruct((batch * seq, d), BF16)],
        scratch_shapes=[pltpu.VMEM((seq, d), F32), pltpu.VMEM((2, row_block, d), BF16),
                        pltpu.SemaphoreType.DMA(()), pltpu.SemaphoreType.DMA((3,))],
        compiler_params=_cparams(("arbitrary", "arbitrary")),
        name="moe_combine_layernorm",
    )(idx_be, y, x, g, b)


def _rope_tables(seq, rot_dim, split_halves=False):
    rows = seq // GRID_W
    row = jnp.repeat(jnp.arange(rows, dtype=jnp.int32), GRID_W).astype(F32)
    col = jnp.tile(jnp.arange(GRID_W, dtype=jnp.int32), rows).astype(F32)
    half = rot_dim // 2
    inv_freq = ROPE_THETA ** (-jnp.arange(0, half, 2, dtype=F32) / half)
    ang = jnp.concatenate([row[:, None] * inv_freq, col[:, None] * inv_freq], axis=-1)
    cos, sin = jnp.cos(ang), jnp.sin(ang)
    if split_halves:
        cos2 = jnp.concatenate([cos, cos], axis=1)
        sin2 = jnp.concatenate([-sin, sin], axis=1)
    else:
        cos2 = jnp.repeat(cos, 2, axis=1)
        sin2 = jnp.stack([-sin, sin], axis=-1).reshape(seq, rot_dim)
    pad = LANES - rot_dim
    if pad:
        cos2 = jnp.concatenate([cos2, jnp.ones((seq, pad), F32)], axis=1)
        sin2 = jnp.concatenate([sin2, jnp.zeros((seq, pad), F32)], axis=1)
    return cos2, sin2


def _moe(x1, xp1, logits_t, w1, w3, w2, ln_g, ln_b, layer, batch, seq):
    n_e = w1.shape[1]
    cap = CAPACITY_FACTOR * seq // n_e
    idx_col, gate_col = _route(logits_t, batch, seq, cap)
    idx_be = idx_col.reshape(batch * n_e, 1, cap)
    idx_eb = idx_col.reshape(batch, n_e, cap).transpose(1, 0, 2).reshape(n_e * batch, 1, cap)
    gate_eb = gate_col.reshape(batch, n_e, cap).transpose(1, 0, 2).reshape(n_e * batch * cap, 1)
    y = _ffn(idx_eb, xp1, w1, w3, w2, layer, gate_eb, batch, seq, cap, tf=256)
    return _combine(idx_be, y, x1, ln_g, ln_b, batch, seq, cap, n_e)


def _split_router(w_router):
    wr = w_router.T
    hi = wr.astype(BF16)
    lo = (wr - hi.astype(F32)).astype(BF16)
    return jnp.concatenate([hi, lo], axis=0)


def _even_mixer(xb, batch, seq, w_in, q_norm, w_uq, kv_norm, w_ukv, gate_w2, gate_b, gla_norm,
                cos_r, sin_r):
    d = w_in.shape[0]
    o_cq, o_ckv, o_kr = 0, MLA_Q_RANK, MLA_Q_RANK + MLA_KV_RANK
    o_gq = o_kr + MLA_ROPE_DIM
    o_gk = o_gq + GLA_HEADS * GLA_DK
    o_gv = o_gk + GLA_HEADS * GLA_DK
    o_gr = o_gv + GLA_HEADS * GLA_DV
    o_lat = o_gr + GLA_HEADS * GLA_DV
    o_end = o_lat + 2 * GLA_GATE_RANK
    zeros = lambda n: jnp.zeros((d, n), BF16)
    tail_pad = LANES - MLA_ROPE_DIM - 2 * GLA_GATE_RANK
    w_in = w_in.astype(BF16)
    w_in_l = jnp.concatenate([
        w_in[:, o_cq:o_ckv], w_in[:, o_gq:o_gk], w_in[:, o_gk:o_gv], w_in[:, o_ckv:o_kr],
        w_in[:, o_kr:o_gq], w_in[:, o_lat:o_end], zeros(tail_pad), zeros(H_GV - H_TAIL - LANES),
        w_in[:, o_gv:o_gr], w_in[:, o_gr:o_lat]], axis=1)
    h = _matmul(xb, w_in_l, F32, tm=1024, tn=1024)

    qk = MLA_NOPE_DIM + MLA_ROPE_DIM
    wuq_l = jnp.pad(w_uq.reshape(MLA_Q_RANK, MLA_HEADS, qk),
                    ((0, 0), (0, 0), (0, MLA_QK_PAD - qk))).reshape(MLA_Q_RANK, MLA_HEADS * MLA_QK_PAD)
    wukv3 = w_ukv.reshape(MLA_KV_RANK, MLA_HEADS, MLA_NOPE_DIM + MLA_V_DIM)
    wukv_l = jnp.concatenate([wukv3[:, :, :MLA_NOPE_DIM].reshape(MLA_KV_RANK, -1),
                              wukv3[:, :, MLA_NOPE_DIM:].reshape(MLA_KV_RANK, -1)], axis=1)
    gw = GLA_HEADS * GLA_DK
    wg = jnp.zeros((LANES, 2 * gw), F32)
    wg = wg.at[MLA_ROPE_DIM:MLA_ROPE_DIM + GLA_GATE_RANK, :gw].set(gate_w2[0])
    wg = wg.at[MLA_ROPE_DIM + GLA_GATE_RANK:MLA_ROPE_DIM + 2 * GLA_GATE_RANK, gw:].set(gate_w2[1])
    gb = jnp.concatenate([gate_b[0], gate_b[1]])[None, :]
    q_p, k_p, v_p, la = _mla_prep(h, q_norm[None, :], wuq_l.astype(BF16), kv_norm[None, :],
                                  wukv_l.astype(BF16), wg.astype(BF16), gb, cos_r, sin_r, seq, tm=256)
    o_mla = _attention(q_p, k_p, v_p, batch=batch, seq=seq, kv_heads=MLA_HEADS, group=1,
                       dqk=MLA_QK_PAD, dv=MLA_V_DIM, q_off=0, k_off=0, v_off=0, tq=4096)
    o_gla = _gla(h, la, gla_norm.reshape(GLA_HEADS, 1, GLA_DV), batch, seq)
    return o_mla, o_gla


def _odd_mixer(xb, batch, seq, w_qkv, q_norm, k_norm, cos_g, sin_g):
    hd = GQA_HEAD_DIM
    scale = hd ** -0.5 * LOG2_E
    n_qk, n_v = (GQA_HEADS + GQA_KV_HEADS) * hd, GQA_KV_HEADS * hd
    split = jnp.concatenate([jnp.arange(0, hd, 2), jnp.arange(1, hd, 2)])
    d_in = w_qkv.shape[0]
    w_qk = w_qkv[:, :n_qk].reshape(d_in, n_qk // hd, hd // 2, 2).transpose(0, 1, 3, 2).reshape(d_in, n_qk)
    w_l = jnp.concatenate([w_qk, w_qkv[:, n_qk:]], axis=1)
    gain_row = jnp.concatenate([jnp.tile(q_norm[split] * scale, GQA_HEADS), jnp.tile(k_norm[split], GQA_KV_HEADS),
                                jnp.ones((n_v,), F32)]).reshape(-1, 1, 4 * hd)
    normed_row = jnp.concatenate([jnp.ones((n_qk,), F32), jnp.zeros((n_v,), F32)]).reshape(-1, 1, 4 * hd)
    gains = jnp.concatenate([gain_row, normed_row], axis=1)
    qkv = _qkv_proj(xb, w_l.astype(BF16), gains, cos_g, sin_g, seq, tm=1024)
    group = GQA_HEADS // GQA_KV_HEADS
    return _attention(qkv, qkv, qkv, batch=batch, seq=seq, kv_heads=GQA_KV_HEADS, group=group,
                      dqk=hd, dv=hd, q_off=0, k_off=GQA_HEADS, v_off=GQA_HEADS + GQA_KV_HEADS, tq=1024)


def kernel(x, mix_w_in, mla_q_norm, mla_w_uq, mla_kv_norm, mla_w_ukv, gla_gate_w2, gla_gate_b,
           gla_out_norm, mix_w_out, gqa_w_qkv, gqa_q_norm, gqa_k_norm, gqa_w_out,
           moe_router, moe_w1, moe_w3, moe_w2, ln_mix_g, ln_mix_b, ln_ffn_g, ln_ffn_b):
    batch, seq, d = x.shape
    cos_r, sin_r = _rope_tables(seq, MLA_ROPE_DIM)
    cos_g, sin_g = _rope_tables(seq, GQA_HEAD_DIM, split_halves=True)
    xf = x.reshape(batch * seq, d)
    xb = xf.astype(BF16)
    n_layers = moe_router.shape[0]
    for layer in range(n_layers):
        i = layer // 2
        wr_split = _split_router(moe_router[layer])
        g_mix, b_mix = ln_mix_g[layer][None, :], ln_mix_b[layer][None, :]
        if layer % 2 == 0:
            o_mla, o_gla = _even_mixer(xb, batch, seq, mix_w_in[i], mla_q_norm[i], mla_w_uq[i],
                                       mla_kv_norm[i], mla_w_ukv[i], gla_gate_w2[i], gla_gate_b[i],
                                       gla_out_norm[i], cos_r, sin_r)
            w_out = mix_w_out[i].astype(BF16)
            n_mla = MLA_HEADS * MLA_V_DIM
            x1, xp1, logits_t = _proj_ln([o_mla, o_gla], [w_out[:n_mla], w_out[n_mla:]], xf,
                                         g_mix, b_mix, wr_split, tm=512)
        else:
            o = _odd_mixer(xb, batch, seq, gqa_w_qkv[i], gqa_q_norm[i], gqa_k_norm[i], cos_g, sin_g)
            x1, xp1, logits_t = _proj_ln([o], [gqa_w_out[i].astype(BF16)], xf,
                                         g_mix, b_mix, wr_split, tm=512)
        xf, xb = _moe(x1, xp1, logits_t, moe_w1, moe_w3, moe_w2, ln_ffn_g[layer][None, :],
                      ln_ffn_b[layer][None, :], layer, batch, seq)
    return xf.reshape(batch, seq, d)
```

```python
import functools

import jax
import jax.numpy as jnp
from jax import lax
from jax.experimental import pallas as pl
from jax.experimental.pallas import tpu as pltpu

F32 = jnp.float32
BF16 = jnp.bfloat16

GRID_W = 64
ROPE_THETA = 10000.0
LN_EPS = 1e-5
RMS_EPS = 1e-6
DEPTH = 4
DEEPNORM_ALPHA = (2.0 * DEPTH) ** 0.25

MLA_HEADS = 8
MLA_Q_RANK = 512
MLA_KV_RANK = 256
MLA_NOPE_DIM = 128
MLA_ROPE_DIM = 64
MLA_V_DIM = 128
MLA_QK_PAD = 256

GLA_HEADS = 4
GLA_DK = 128
GLA_DV = 256
GLA_GATE_RANK = 16
GLA_GATE_TAU = 16.0
GLA_CHUNK = 64
GLA_GROUP = 256
GLA_SCAN_UNROLL = 8
GLA_GROUP_UNROLL = 4

GQA_HEADS = 16
GQA_KV_HEADS = 4
GQA_HEAD_DIM = 128

N_EXPERTS = 16
CAPACITY_FACTOR = 2

LANES = 128
VMEM_LIMIT_BYTES = 56 * 1024 * 1024

LOG2_E = 1.4426950408889634
ATTN_SAFE_LOG2_RANGE = 60.0

H_CQ = 0
H_GQ = 512
H_GK = 1024
H_CKV = 1536
H_TAIL = 1792
H_GV = 2048
H_GR = 3072
H_WIDTH = 4096


def _cparams(sem):
    return pltpu.CompilerParams(dimension_semantics=sem, vmem_limit_bytes=VMEM_LIMIT_BYTES)


def _nt_dot(a, b):
    return lax.dot_general(a, b, (((1,), (1,)), ((), ())), preferred_element_type=F32)


def _tn_dot(a, b):
    return lax.dot_general(a, b, (((0,), (0,)), ((), ())), preferred_element_type=F32)


def _rope(x, cos, sin_signed):
    lane = lax.broadcasted_iota(jnp.int32, x.shape, 1)
    partner = jnp.where((lane & 1) == 0, pltpu.roll(x, LANES - 1, 1), pltpu.roll(x, 1, 1))
    return x * cos + partner * sin_signed


def _rms(x, gain):
    return x * lax.rsqrt(jnp.mean(x * x, axis=-1, keepdims=True) + RMS_EPS) * gain


def _store_token_major(ref, val, first_row=0):
    count, n = val.shape[0], val.shape[1] // LANES
    for j in range(n):
        ref[pl.ds(first_row * n + j, count, stride=n), :] = val[:, j * LANES:(j + 1) * LANES]


def _layer_norm(z, g, b):
    mu = jnp.mean(z, axis=-1, keepdims=True)
    zc = z - mu
    var = jnp.mean(zc * zc, axis=-1, keepdims=True)
    return zc * lax.rsqrt(var + LN_EPS) * g + b


def _mm_kernel(x_ref, w_ref, o_ref):
    o_ref[...] = jnp.dot(x_ref[...], w_ref[...], preferred_element_type=F32).astype(o_ref.dtype)


def _matmul(x, w, out_dtype, tm, tn):
    m, k = x.shape
    n = w.shape[1]
    tm, tn = min(tm, m), min(tn, n)
    return pl.pallas_call(
        _mm_kernel,
        grid=(n // tn, m // tm),
        in_specs=[pl.BlockSpec((tm, k), lambda j, i: (i, 0)),
                  pl.BlockSpec((k, tn), lambda j, i: (0, j))],
        out_specs=pl.BlockSpec((tm, tn), lambda j, i: (i, j)),
        out_shape=jax.ShapeDtypeStruct((m, n), out_dtype),
        compiler_params=_cparams(("parallel", "parallel")),
        name="dense_matmul",
    )(x, w)


def _qkv_kernel(x_ref, w_ref, g_ref, ones_ref, cos_ref, sin_ref, o_ref, *, heads_per_tile):
    acc = jnp.dot(x_ref[...], w_ref[...], preferred_element_type=F32)
    sq = acc * acc
    sq_hi = sq.astype(BF16)
    sq_lo = (sq - sq_hi.astype(F32)).astype(BF16)
    ones_bd = ones_ref[...]
    ms = (jnp.dot(sq_hi, ones_bd, preferred_element_type=F32)
          + jnp.dot(sq_lo, ones_bd, preferred_element_type=F32)) * (1.0 / GQA_HEAD_DIM)
    y = acc * lax.rsqrt(ms + RMS_EPS) * g_ref[0:1, :]
    cos = cos_ref[...]
    sin = sin_ref[...]
    for hd in range(heads_per_tile):
        sl = slice(hd * GQA_HEAD_DIM, (hd + 1) * GQA_HEAD_DIM)
        yh = y[:, sl]
        rot = yh * cos + pltpu.roll(yh, GQA_HEAD_DIM // 2, 1) * sin
        o_ref[:, sl] = jnp.where(g_ref[1:2, sl] > 0.5, rot, acc[:, sl]).astype(o_ref.dtype)


def _qkv_proj(xb, w, gains, cos, sin, seq, tm):
    m, k = xb.shape
    n = w.shape[1]
    tn = 4 * GQA_HEAD_DIM
    tm = min(tm, seq)
    n_tiles = n // tn
    pos_blocks = seq // tm
    kern = functools.partial(_qkv_kernel, heads_per_tile=4)
    col_head = jnp.arange(tn, dtype=jnp.int32) // GQA_HEAD_DIM
    ones_bd = (col_head[:, None] == col_head[None, :]).astype(BF16)
    return pl.pallas_call(
        kern,
        grid=(n_tiles, m // tm),
        in_specs=[pl.BlockSpec((tm, k), lambda j, i: (i, 0)),
                  pl.BlockSpec((k, tn), lambda j, i: (0, j)),
                  pl.BlockSpec((None, 2, tn), lambda j, i: (j, 0, 0)),
                  pl.BlockSpec((tn, tn), lambda j, i: (0, 0)),
                  pl.BlockSpec((tm, LANES), lambda j, i: (i % pos_blocks, 0)),
                  pl.BlockSpec((tm, LANES), lambda j, i: (i % pos_blocks, 0))],
        out_specs=pl.BlockSpec((tm, tn), lambda j, i: (i, j)),
        out_shape=jax.ShapeDtypeStruct((m, n), BF16),
        compiler_params=_cparams(("parallel", "parallel")),
        name="gqa_qkv_proj",
    )(xb, w, gains, ones_bd, cos, sin)


def _attn_kernel(q_ref, k_ref, v_ref, o_ref, vx_s, kmax_s, *, group, dqk, dv, kc_single, kc_online):
    tq = q_ref.shape[0]
    seq = k_ref.shape[0]
    rows = group * tq

    @pl.when(pl.program_id(2) == 0)
    def _():
        lane = lax.broadcasted_iota(jnp.int32, (seq, LANES), 1)
        vx_s[:, :dv] = v_ref[...]
        vx_s[:, dv:] = jnp.where(lane == 0, 1.0, 0.0).astype(BF16)
        k32 = k_ref[...].astype(F32)
        k_sq = jnp.sum(k32 * k32, axis=1, keepdims=True)
        kmax_s[...] = jnp.max(k_sq, axis=0, keepdims=True)

    q = jnp.concatenate([q_ref[:, g * dqk:(g + 1) * dqk] for g in range(group)], axis=0)
    q32 = q.astype(F32)
    bound = jnp.sqrt(jnp.sum(q32 * q32, axis=1, keepdims=True) * kmax_s[...])
    single_pass = jnp.max(bound) <= ATTN_SAFE_LOG2_RANGE

    def write(acc):
        o = acc[:, :dv] / acc[:, dv:dv + 1]
        for g in range(group):
            o_ref[:, g * dv:(g + 1) * dv] = o[g * tq:(g + 1) * tq, :].astype(o_ref.dtype)

    @pl.when(single_pass)
    def _():
        kc = kc_single
        acc = jnp.zeros((rows, dv + LANES), F32)
        for c in range(seq // kc):
            p = jnp.exp2(_nt_dot(q, k_ref[c * kc:(c + 1) * kc, :]) - bound)
            acc = acc + jnp.dot(p.astype(BF16), vx_s[c * kc:(c + 1) * kc, :], preferred_element_type=F32)
        write(acc)

    @pl.when(jnp.logical_not(single_pass))
    def _():
        kc = kc_online
        m = jnp.full((rows, 1), -jnp.inf, F32)
        acc = jnp.zeros((rows, dv + LANES), F32)
        for c in range(seq // kc):
            s = _nt_dot(q, k_ref[c * kc:(c + 1) * kc, :])
            m_new = jnp.maximum(m, jnp.max(s, axis=1, keepdims=True))
            p = jnp.exp2(s - m_new)
            acc = acc * jnp.exp2(m - m_new) + jnp.dot(p.astype(BF16), vx_s[c * kc:(c + 1) * kc, :],
                                                      preferred_element_type=F32)
            m = m_new
        write(acc)


def _attention(q_arr, k_arr, v_arr, *, batch, seq, kv_heads, group, dqk, dv,
               q_off, k_off, v_off, tq):
    tq = min(tq, seq)
    nq = seq // tq
    kern = functools.partial(_attn_kernel, group=group, dqk=dqk, dv=dv,
                             kc_single=min(256, seq), kc_online=min(1024, seq))
    return pl.pallas_call(
        kern,
        grid=(batch, kv_heads, nq),
        in_specs=[pl.BlockSpec((tq, group * dqk), lambda b, h, i: (b * nq + i, q_off + h)),
                  pl.BlockSpec((seq, dqk), lambda b, h, i: (b, k_off + h)),
                  pl.BlockSpec((seq, dv), lambda b, h, i: (b, v_off + h))],
        out_specs=pl.BlockSpec((tq, group * dv), lambda b, h, i: (b * nq + i, h)),
        out_shape=jax.ShapeDtypeStruct((batch * seq, kv_heads * group * dv), BF16),
        scratch_shapes=[pltpu.VMEM((seq, dv + LANES), BF16), pltpu.VMEM((1, 1), F32)],
        compiler_params=_cparams(("arbitrary", "arbitrary", "arbitrary")),
        name="softmax_attention",
    )(q_arr, k_arr, v_arr)


def _mla_prep_kernel(cq_ref, ckv_ref, tail_ref, qn_ref, wuq_ref, kvn_ref, wukv_ref, wg_ref, gb_ref,
                     cos_ref, sin_ref, q_ref, k_ref, v_ref, la_ref):
    cos = cos_ref[...]
    sin = sin_ref[...]
    scale = (MLA_NOPE_DIM + MLA_ROPE_DIM) ** -0.5 * LOG2_E
    q = jnp.dot(_rms(cq_ref[...], qn_ref[...]).astype(BF16), wuq_ref[...],
                preferred_element_type=F32) * scale
    kv = jnp.dot(_rms(ckv_ref[...], kvn_ref[...]).astype(BF16), wukv_ref[...],
                 preferred_element_type=F32)
    tail = tail_ref[...]
    lane = lax.broadcasted_iota(jnp.int32, tail.shape, 1)
    k_pe = jnp.where(lane < MLA_ROPE_DIM, _rope(tail, cos, sin), 0.0).astype(BF16)
    for hd in range(MLA_HEADS):
        c0 = hd * MLA_QK_PAD
        q_ref[:, c0:c0 + LANES] = q[:, c0:c0 + LANES].astype(BF16)
        q_ref[:, c0 + LANES:c0 + 2 * LANES] = _rope(q[:, c0 + LANES:c0 + 2 * LANES], cos, sin).astype(BF16)
        k_ref[:, c0:c0 + LANES] = kv[:, hd * LANES:(hd + 1) * LANES].astype(BF16)
        k_ref[:, c0 + LANES:c0 + 2 * LANES] = k_pe
    nv = MLA_HEADS * MLA_V_DIM
    v_ref[...] = kv[:, nv:].astype(BF16)
    gate = jnp.dot(tail.astype(BF16), wg_ref[...], preferred_element_type=F32) + gb_ref[...]
    log_sig = jnp.minimum(gate, 0.0) - jnp.log1p(jnp.exp(-jnp.abs(gate)))
    la_ref[...] = log_sig / GLA_GATE_TAU


def _mla_prep(h, qn, wuq, kvn, wukv, wg, gb, cos, sin, seq, tm):
    m = h.shape[0]
    tm = min(tm, seq)
    pos_blocks = seq // tm
    qw = MLA_HEADS * MLA_QK_PAD
    vw = MLA_HEADS * MLA_V_DIM
    gw = 2 * GLA_HEADS * GLA_DK
    const = lambda i: (0, 0)
    return pl.pallas_call(
        _mla_prep_kernel,
        grid=(m // tm,),
        in_specs=[pl.BlockSpec((tm, MLA_Q_RANK), lambda i: (i, H_CQ // MLA_Q_RANK)),
                  pl.BlockSpec((tm, MLA_KV_RANK), lambda i: (i, H_CKV // MLA_KV_RANK)),
                  pl.BlockSpec((tm, LANES), lambda i: (i, H_TAIL // LANES)),
                  pl.BlockSpec((1, MLA_Q_RANK), const),
                  pl.BlockSpec((MLA_Q_RANK, qw), const),
                  pl.BlockSpec((1, MLA_KV_RANK), const),
                  pl.BlockSpec((MLA_KV_RANK, 2 * vw), const),
                  pl.BlockSpec((LANES, gw), const),
                  pl.BlockSpec((1, gw), const),
                  pl.BlockSpec((tm, LANES), lambda i: (i % pos_blocks, 0)),
                  pl.BlockSpec((tm, LANES), lambda i: (i % pos_blocks, 0))],
        out_specs=[pl.BlockSpec((tm, qw), lambda i: (i, 0)),
                   pl.BlockSpec((tm, qw), lambda i: (i, 0)),
                   pl.BlockSpec((tm, vw), lambda i: (i, 0)),
                   pl.BlockSpec((tm, gw), lambda i: (i, 0))],
        out_shape=[jax.ShapeDtypeStruct((m, qw), BF16),
                   jax.ShapeDtypeStruct((m, qw), BF16),
                   jax.ShapeDtypeStruct((m, vw), BF16),
                   jax.ShapeDtypeStruct((m, gw), F32)],
        compiler_params=_cparams(("parallel",)),
        name="mla_prep",
    )(h, h, h, qn, wuq, kvn, wukv, wg, gb, cos, sin)


def _split3(x):
    hi = x.astype(BF16)
    r1 = x - hi.astype(F32)
    mid = r1.astype(BF16)
    lo = (r1 - mid.astype(F32)).astype(BF16)
    return hi, mid, lo


def _gla_kernel(q_ref, k_ref, v_ref, gr_ref, laf_ref, lab_ref, gn_ref, o_ref,
                of_s, ob_s, qf_s, qb_s, kf_s, kb_s, df_s, db_s, stf_s, stb_s):
    seq = q_ref.shape[0]
    c = GLA_CHUNK
    r = min(GLA_GROUP, seq)
    cpg = r // c
    n_groups = seq // r
    group_unroll = min(GLA_GROUP_UNROLL, n_groups)
    scan_unroll = min(GLA_SCAN_UNROLL, seq // c)
    n_chunks = seq // c
    scale = GLA_DK ** -0.5

    row = lax.broadcasted_iota(jnp.int32, (r, r), 0)
    col = lax.broadcasted_iota(jnp.int32, (r, r), 1)
    same = (row // c) == (col // c)
    tril = same & (col <= row)
    triu = same & (col >= row)
    tril_b = tril.astype(F32).astype(BF16)
    triu_b = triu.astype(F32).astype(BF16)

    def group_pair(gp, carry):
        for u in range(group_unroll):
            group_body(gp * group_unroll + u)
        return carry

    def group_body(gi):
        r0 = pl.multiple_of(gi * r, r)
        rows = pl.ds(r0, r)
        q = q_ref[rows, :] * scale
        k = k_ref[rows, :]
        v = v_ref[rows, :].astype(BF16)
        for la_ref, mask, mask_b, edge, o_s, q_s, k_s, d_s in (
                (laf_ref, tril, tril_b, c - 1, of_s, qf_s, kf_s, df_s),
                (lab_ref, triu, triu_b, 0, ob_s, qb_s, kb_s, db_s)):
            la = la_ref[rows, :]
            la_hi = la.astype(BF16)
            la_lo = (la - la_hi.astype(F32)).astype(BF16)
            both = jnp.dot(mask_b, jnp.concatenate([la_hi, la_lo], axis=1), preferred_element_type=F32)
            b = both[:, :GLA_DK] + both[:, GLA_DK:]
            b_edge = jnp.concatenate(
                [jnp.broadcast_to(b[ci * c + edge:ci * c + edge + 1, :], (c, GLA_DK)) for ci in range(cpg)],
                axis=0)
            q_in = (q * jnp.exp(b)).astype(BF16)
            k_in = (k * jnp.exp(-b)).astype(BF16)
            k_st = (k * jnp.exp(b_edge - b)).astype(BF16)
            att = jnp.where(mask, _nt_dot(q_in, k_in), 0.0)
            o_s[rows, :] = jnp.dot(att.astype(BF16), v, preferred_element_type=F32)
            q_s[rows, :] = q_in
            k_s[rows, :] = k_st
            for ci in range(cpg):
                d_s[pl.ds(gi * cpg + ci, 1), :] = jnp.exp(b[ci * c + edge:ci * c + edge + 1, :])

    lax.fori_loop(0, n_groups // group_unroll, group_pair, 0)

    stf_s[...] = jnp.zeros_like(stf_s)
    stb_s[...] = jnp.zeros_like(stb_s)

    def chunk_body(i, carry):
        for u in range(scan_unroll):
            fwd_n = i * scan_unroll + u
            for n, o_s, q_s, k_s, d_s, st_s in ((fwd_n, of_s, qf_s, kf_s, df_s, stf_s),
                                                (n_chunks - 1 - fwd_n, ob_s, qb_s, kb_s, db_s, stb_s)):
                rows = pl.ds(pl.multiple_of(n * c, c), c)
                st = st_s[...]
                o_s[rows, :] += _nt_dot(q_s[rows, :], st.astype(BF16))
                ds = _tn_dot(v_ref[rows, :].astype(BF16), k_s[rows, :])
                st_s[...] = d_s[pl.ds(n, 1), :] * st + ds
        return carry

    lax.fori_loop(0, n_chunks // scan_unroll, chunk_body, 0)

    gain = gn_ref[...]

    def out_body(gi, carry):
        rows = pl.ds(pl.multiple_of(gi * r, r), r)
        o = _rms(of_s[rows, :] + ob_s[rows, :], gain)
        gr = gr_ref[rows, :]
        o_ref[rows, :] = (o * (gr * jax.nn.sigmoid(gr))).astype(o_ref.dtype)
        return carry

    lax.fori_loop(0, n_groups, out_body, 0)


def _gla(h, la, gn, batch, seq):
    dk, dv, nh = GLA_DK, GLA_DV, GLA_HEADS
    n_chunks = seq // GLA_CHUNK
    return pl.pallas_call(
        _gla_kernel,
        grid=(batch, nh),
        in_specs=[pl.BlockSpec((seq, dk), lambda b, hd: (b, H_GQ // dk + hd)),
                  pl.BlockSpec((seq, dk), lambda b, hd: (b, H_GK // dk + hd)),
                  pl.BlockSpec((seq, dv), lambda b, hd: (b, H_GV // dv + hd)),
                  pl.BlockSpec((seq, dv), lambda b, hd: (b, H_GR // dv + hd)),
                  pl.BlockSpec((seq, dk), lambda b, hd: (b, hd)),
                  pl.BlockSpec((seq, dk), lambda b, hd: (b, nh + hd)),
                  pl.BlockSpec((None, 1, dv), lambda b, hd: (hd, 0, 0))],
        out_specs=pl.BlockSpec((seq, dv), lambda b, hd: (b, hd)),
        out_shape=jax.ShapeDtypeStruct((batch * seq, nh * dv), BF16),
        scratch_shapes=[pltpu.VMEM((seq, dv), F32), pltpu.VMEM((seq, dv), F32),
                        pltpu.VMEM((seq, dk), BF16), pltpu.VMEM((seq, dk), BF16),
                        pltpu.VMEM((seq, dk), BF16), pltpu.VMEM((seq, dk), BF16),
                        pltpu.VMEM((n_chunks, dk), F32), pltpu.VMEM((n_chunks, dk), F32),
                        pltpu.VMEM((dv, dk), F32), pltpu.VMEM((dv, dk), F32)],
        compiler_params=_cparams(("arbitrary", "arbitrary")),
        name="gla_bidirectional",
    )(h, h, h, h, la, la, gn)


PROJ_SUBTILES = 1


def _proj_ln_kernel(*refs, n_in):
    a_refs = refs[:n_in]
    w_refs = refs[n_in:2 * n_in]
    x_ref, g_ref, b_ref, wr_ref, xo_ref, xp_ref, lg_ref = refs[2 * n_in:]
    tm, d = x_ref.shape
    half = d // 2
    sub = tm // PROJ_SUBTILES
    wr = wr_ref[...]
    n_e = wr.shape[0] // 2
    for t in range(PROJ_SUBTILES):
        rows = slice(t * sub, (t + 1) * sub)
        mix = jnp.dot(a_refs[0][rows, :], w_refs[0][...], preferred_element_type=F32)
        for a_ref, w_ref in zip(a_refs[1:], w_refs[1:]):
            mix = mix + jnp.dot(a_ref[rows, :], w_ref[...], preferred_element_type=F32)
        y = _layer_norm(DEEPNORM_ALPHA * x_ref[rows, :] + mix, g_ref[...], b_ref[...])
        xo_ref[rows, :] = y
        y_hi = y.astype(BF16)
        y_hi32 = y_hi.astype(F32)
        bits = lax.bitcast_convert_type(y_hi32, jnp.int32)
        _store_token_major(xp_ref, lax.shift_right_logical(bits[:, :half], 16) | bits[:, half:],
                           first_row=t * sub)
        y_lo = (y - y_hi32).astype(BF16)
        by_hi = _nt_dot(wr, y_hi)
        lg_ref[:, rows] = (by_hi[:n_e, :] + by_hi[n_e:, :]) + _nt_dot(wr[:n_e, :], y_lo)


def _proj_ln(acts, weights, x, g, b, wr_split, tm):
    m, d = x.shape
    tm = min(tm, m)
    n_in = len(acts)
    n_e = wr_split.shape[0] // 2
    np_ = d // (2 * LANES)
    const = lambda i: (0, 0)
    in_specs = ([pl.BlockSpec((tm, a.shape[1]), lambda i: (i, 0)) for a in acts]
                + [pl.BlockSpec(w.shape, const) for w in weights]
                + [pl.BlockSpec((tm, d), lambda i: (i, 0)),
                   pl.BlockSpec((1, d), const), pl.BlockSpec((1, d), const),
                   pl.BlockSpec((2 * n_e, d), const)])
    return pl.pallas_call(
        functools.partial(_proj_ln_kernel, n_in=n_in),
        grid=(m // tm,),
        in_specs=in_specs,
        out_specs=[pl.BlockSpec((tm, d), lambda i: (i, 0)),
                   pl.BlockSpec((tm * np_, LANES), lambda i: (i, 0)),
                   pl.BlockSpec((n_e, tm), lambda i: (0, i))],
        out_shape=[jax.ShapeDtypeStruct((m, d), F32),
                   jax.ShapeDtypeStruct((m * np_, LANES), jnp.int32),
                   jax.ShapeDtypeStruct((n_e, m), F32)],
        compiler_params=_cparams(("parallel",)),
        name="out_proj_layernorm_router",
    )(*acts, *weights, x, g, b, wr_split)


def _route_kernel(lg_ref, idx_ref, gate_ref, aff_s, cum_s, start_s, *, cap):
    lg = lg_ref[...]
    ex = jnp.exp(lg - jnp.max(lg, axis=0, keepdims=True))
    aff = ex / jnp.sum(ex, axis=0, keepdims=True)
    bits = lax.bitcast_convert_type(aff, jnp.int32)
    n_e, seq = lg.shape
    nblk = seq // LANES
    capf = float(cap)

    def bisect(_, carry):
        lo, hi = carry
        mid = lo + ((hi - lo + 1) >> 1)
        cnt = jnp.sum(jnp.where(bits >= mid, 1.0, 0.0), axis=1, keepdims=True)
        ok = cnt >= capf
        return jnp.where(ok, mid, lo), jnp.where(ok, hi, mid - 1)

    lo0 = jnp.zeros((n_e, 1), jnp.int32)
    hi0 = jnp.full((n_e, 1), 0x7F800000, jnp.int32)
    thr, _ = lax.fori_loop(0, 32, bisect, (lo0, hi0))

    gt = bits > thr
    eq = bits == thr
    need = capf - jnp.sum(jnp.where(gt, 1.0, 0.0), axis=1, keepdims=True)

    r_i = lax.broadcasted_iota(jnp.int32, (LANES, LANES), 0)
    c_i = lax.broadcasted_iota(jnp.int32, (LANES, LANES), 1)
    upper = (r_i <= c_i).astype(F32).astype(BF16)
    eq_off = jnp.zeros((n_e, 1), F32)
    sel_off = jnp.zeros((n_e, 1), F32)
    for j in range(nblk):
        blk = slice(j * LANES, (j + 1) * LANES)
        eq01 = jnp.where(eq[:, blk], 1.0, 0.0)
        eq_cum = jnp.dot(eq01.astype(BF16), upper, preferred_element_type=F32) + eq_off
        eq_off = eq_cum[:, LANES - 1:LANES]
        tie_rank = eq_cum - eq01
        sel01 = jnp.where(gt[:, blk] | (eq[:, blk] & (tie_rank < need)), 1.0, 0.0)
        sel_cum = jnp.dot(sel01.astype(BF16), upper, preferred_element_type=F32) + sel_off
        rows = slice(j * n_e, (j + 1) * n_e)
        cum_s[rows, :] = sel_cum
        start_s[rows, :] = jnp.broadcast_to(sel_off, (n_e, LANES))
        aff_s[rows, :] = aff[:, blk]
        sel_off = sel_cum[:, LANES - 1:LANES]

    slot_row = lax.broadcasted_iota(jnp.int32, (1, cap), 1).astype(F32)
    slot_col = lax.broadcasted_iota(jnp.int32, (cap, 1), 0).astype(F32)
    lane = lax.broadcasted_iota(jnp.int32, (cap, LANES), 1).astype(F32)
    blk_id = lax.broadcasted_iota(jnp.int32, (nblk, LANES), 0).astype(F32).astype(BF16)

    def extract(e, carry):
        rows = pl.ds(e, nblk, stride=n_e)
        cum_e = cum_s[rows, :]
        start_e = start_s[rows, :][:, 0:1]
        end_e = cum_e[:, LANES - 1:LANES]
        in_blk = jnp.where((start_e <= slot_row) & (slot_row < end_e), 1.0, 0.0).astype(BF16)
        cum_hi = jnp.floor(cum_e * (1.0 / 32.0))
        cum_lo = cum_e - 32.0 * cum_hi
        g = 32.0 * _tn_dot(in_blk, cum_hi.astype(BF16)) + _tn_dot(in_blk, cum_lo.astype(BF16))
        blk_of_slot = _tn_dot(in_blk, blk_id)[:, 0:1]
        pos = jnp.sum(jnp.where(g <= slot_col, 1.0, 0.0), axis=1, keepdims=True)
        a_hi, a_mid, a_lo = _split3(aff_s[rows, :])
        a = (_tn_dot(in_blk, a_hi) + _tn_dot(in_blk, a_mid)) + _tn_dot(in_blk, a_lo)
        gate = jnp.sum(jnp.where(lane == pos, a, 0.0), axis=1, keepdims=True)
        out_rows = pl.ds(pl.multiple_of(e * cap, cap), cap)
        idx_ref[out_rows, :] = (float(LANES) * blk_of_slot + pos).astype(jnp.int32)
        gate_ref[out_rows, :] = gate
        return carry

    lax.fori_loop(0, n_e, extract, 0)


def _route(logits_t, batch, seq, cap):
    n_e = logits_t.shape[0]
    kern = functools.partial(_route_kernel, cap=cap)
    scratch_rows = (seq // LANES) * n_e
    return pl.pallas_call(
        kern,
        grid=(batch,),
        in_specs=[pl.BlockSpec((n_e, seq), lambda b: (0, b))],
        out_specs=[pl.BlockSpec((n_e * cap, 1), lambda b: (b, 0)),
                   pl.BlockSpec((n_e * cap, 1), lambda b: (b, 0))],
        out_shape=[jax.ShapeDtypeStruct((batch * n_e * cap, 1), jnp.int32),
                   jax.ShapeDtypeStruct((batch * n_e * cap, 1), F32)],
        scratch_shapes=[pltpu.VMEM((scratch_rows, LANES), F32), pltpu.VMEM((scratch_rows, LANES), F32),
                        pltpu.VMEM((scratch_rows, LANES), F32)],
        compiler_params=_cparams(("parallel",)),
        name="expert_choice_route",
    )(logits_t)


GATHER_UNROLL = 32


def _ffn_kernel(idx_cur, idx_nxt, xp_hbm, w1_ref, w3_ref, w2_ref, gate_ref, o_ref,
                land, x_lo, x_hi, sem, *, cap, seq, bpt, issue_steps, unpack_rows):
    m_tiles = pl.num_programs(1)
    f = pl.program_id(2)
    t = pl.program_id(0) * m_tiles + pl.program_id(1)
    n_tiles = pl.num_programs(0) * m_tiles
    slot = t % 2
    tm = bpt * cap
    half = x_lo.shape[1]
    npk = half // LANES

    def row_copy(dst_slot, r, src_row):
        return pltpu.make_async_copy(xp_hbm.at[pl.ds(pl.multiple_of(src_row * npk, npk), npk), :],
                                     land.at[dst_slot, pl.ds(pl.multiple_of(r * npk, npk), npk), :],
                                     sem.at[dst_slot])

    def issue(idx_ref, tile, dst_slot, bb, s0, n):
        src_base = ((tile % m_tiles) * bpt + bb) * seq
        dst_base = bb * cap

        def body(i, carry):
            for u in range(GATHER_UNROLL):
                s = s0 + i * GATHER_UNROLL + u
                row_copy(dst_slot, dst_base + s, src_base + idx_ref[bb, 0, s]).start()
            return carry

        lax.fori_loop(0, n // GATHER_UNROLL, body, 0)

    @pl.when((t == 0) & (f == 0))
    def _():
        for bb in range(bpt):
            issue(idx_cur, t, slot, bb, 0, cap)

    @pl.when(f == 0)
    def _():
        pltpu.make_async_copy(xp_hbm.at[pl.ds(0, tm * npk), :], land.at[slot], sem.at[slot]).wait()

        def unpack(i, carry):
            r0 = pl.multiple_of(i * unpack_rows, unpack_rows)
            rows = pl.ds(r0, unpack_rows)
            for j in range(npk):
                w = land[slot, pl.ds(r0 * npk + j, unpack_rows, stride=npk), :]
                cols = slice(j * LANES, (j + 1) * LANES)
                x_lo[rows, cols] = lax.bitcast_convert_type(w << 16, F32).astype(BF16)
                x_hi[rows, cols] = lax.bitcast_convert_type(w & jnp.int32(-65536), F32).astype(BF16)
            return carry

        lax.fori_loop(0, tm // unpack_rows, unpack, 0)
        o_ref[...] = jnp.zeros_like(o_ref)

    per_batch = issue_steps // bpt
    n = cap // per_batch
    for bb in range(bpt):
        @pl.when((f >= bb * per_batch) & (f < (bb + 1) * per_batch) & (t + 1 < n_tiles))
        def _(bb=bb):
            issue(idx_nxt, t + 1, 1 - slot, bb, (f - bb * per_batch) * n, n)

    w1 = w1_ref[...].astype(BF16)
    w3 = w3_ref[...].astype(BF16)
    lo, hi = x_lo[...], x_hi[...]
    h1 = (jnp.dot(lo, w1[:half], preferred_element_type=F32)
          + jnp.dot(hi, w1[half:], preferred_element_type=F32))
    h3 = (jnp.dot(lo, w3[:half], preferred_element_type=F32)
          + jnp.dot(hi, w3[half:], preferred_element_type=F32))
    hdn = (h1 * jax.nn.sigmoid(h1)) * h3
    part = jnp.dot(hdn.astype(BF16), w2_ref[...].astype(BF16), preferred_element_type=F32)
    is_last = f == pl.num_programs(2) - 1
    gate = jnp.where(is_last, gate_ref[...], 1.0)
    o_ref[...] = (o_ref[...] + part) * gate


def _ffn(idx_eb, xp, w1, w3, w2, layer, gate_col, batch, seq, cap, tf):
    _, n_e, d, ff = w1.shape
    bpt = min(2, batch)
    tm = bpt * cap
    mt = batch // bpt
    tf = min(tf, ff // bpt)
    nf = ff // tf
    assert nf >= 2 and ff % tf == 0
    issue_steps = bpt * min(2, nf // bpt)
    assert cap % (issue_steps // bpt * GATHER_UNROLL) == 0
    n_tiles = n_e * mt
    kern = functools.partial(_ffn_kernel, cap=cap, seq=seq, bpt=bpt, issue_steps=issue_steps,
                             unpack_rows=min(256, tm))
    smem_idx = lambda fn: pl.BlockSpec((bpt, 1, cap), fn, memory_space=pltpu.SMEM)
    return pl.pallas_call(
        kern,
        grid=(n_e, mt, nf),
        in_specs=[smem_idx(lambda e, m, f: (e * mt + m, 0, 0)),
                  smem_idx(lambda e, m, f: (jnp.minimum(e * mt + m + 1, n_tiles - 1), 0, 0)),
                  pl.BlockSpec(memory_space=pl.ANY),
                  pl.BlockSpec((None, None, d, tf), lambda e, m, f: (layer, e, 0, f)),
                  pl.BlockSpec((None, None, d, tf), lambda e, m, f: (layer, e, 0, f)),
                  pl.BlockSpec((None, None, tf, d), lambda e, m, f: (layer, e, f, 0)),
                  pl.BlockSpec((tm, 1), lambda e, m, f: (e * mt + m, 0))],
        out_specs=pl.BlockSpec((tm, d), lambda e, m, f: (e * mt + m, 0)),
        out_shape=jax.ShapeDtypeStruct((n_e * batch * cap, d), F32),
        scratch_shapes=[pltpu.VMEM((2, tm * (d // (2 * LANES)), LANES), jnp.int32),
                        pltpu.VMEM((tm, d // 2), BF16), pltpu.VMEM((tm, d // 2), BF16),
                        pltpu.SemaphoreType.DMA((2,))],
        compiler_params=_cparams(("arbitrary", "arbitrary", "arbitrary")),
        name="moe_expert_ffn",
    )(idx_eb, idx_eb, xp, w1, w3, w2, gate_col)


COMBINE_UNROLL = 8


def _combine_kernel(idx_ref, y_ref, x_hbm, g_ref, b_ref, xo_hbm, xb_hbm, acc, xb_stage, sem, out_sem,
                    *, cap, seq, n_e, row_block):
    b = pl.program_id(0)
    e = pl.program_id(1)

    @pl.when(e == 0)
    def _():
        cp = pltpu.make_async_copy(x_hbm.at[pl.ds(b * seq, seq), :], acc, sem)
        cp.start()
        cp.wait()

        def scale(i, carry):
            rows = pl.ds(pl.multiple_of(i * row_block, row_block), row_block)
            acc[rows, :] = acc[rows, :] * DEEPNORM_ALPHA
            return carry

        lax.fori_loop(0, seq // row_block, scale, 0)

    def add_rows(i, carry):
        s0 = i * COMBINE_UNROLL
        toks = [idx_ref[0, 0, s0 + u] for u in range(COMBINE_UNROLL)]
        sums = [acc[pl.ds(toks[u], 1), :] + y_ref[pl.ds(s0 + u, 1), :] for u in range(COMBINE_UNROLL)]
        for u in range(COMBINE_UNROLL):
            acc[pl.ds(toks[u], 1), :] = sums[u]
        return carry

    lax.fori_loop(0, cap // COMBINE_UNROLL, add_rows, 0)

    @pl.when(e == n_e - 1)
    def _():
        n_blk = seq // row_block

        def f32_copy(i):
            return pltpu.make_async_copy(acc.at[pl.ds(i * row_block, row_block), :],
                                         xo_hbm.at[pl.ds(b * seq + i * row_block, row_block), :],
                                         out_sem.at[0])

        def bf16_copy(i, slot):
            return pltpu.make_async_copy(xb_stage.at[slot],
                                         xb_hbm.at[pl.ds(b * seq + i * row_block, row_block), :],
                                         out_sem.at[1 + slot])

        def ln_block(i, carry):
            rows = pl.ds(pl.multiple_of(i * row_block, row_block), row_block)
            y = _layer_norm(acc[rows, :], g_ref[...], b_ref[...])
            acc[rows, :] = y
            f32_copy(i).start()
            slot = i % 2

            @pl.when(i >= 2)
            def _():
                bf16_copy(i - 2, slot).wait()

            xb_stage[slot] = y.astype(BF16)
            bf16_copy(i, slot).start()
            return carry

        lax.fori_loop(0, n_blk, ln_block, 0)
        for i in range(max(n_blk - 2, 0), n_blk):
            bf16_copy(i, i % 2).wait()

        def drain(i, carry):
            f32_copy(i).wait()
            return carry

        lax.fori_loop(0, n_blk, drain, 0)


def _combine(idx_be, y, x, g, b, batch, seq, cap, n_e):
    d = x.shape[1]
    row_block = min(256, seq)
    kern = functools.partial(_combine_kernel, cap=cap, seq=seq, n_e=n_e, row_block=row_block)
    const = lambda b, e: (0, 0)
    return pl.pallas_call(
        kern,
        grid=(batch, n_e),
        in_specs=[pl.BlockSpec((1, 1, cap), lambda b, e: (b * n_e + e, 0, 0), memory_space=pltpu.SMEM),
                  pl.BlockSpec((cap, d), lambda b, e: (e * batch + b, 0)),
                  pl.BlockSpec(memory_space=pl.ANY),
                  pl.BlockSpec((1, d), const), pl.BlockSpec((1, d), const)],
        out_specs=[pl.BlockSpec(memory_space=pl.ANY), pl.BlockSpec(memory_space=pl.ANY)],
        out_shape=[jax.ShapeDtypeStruct((batch * seq, d), F32),
                   jax.ShapeDtypeStruct((batch * seq, d), BF16)],
        scratch_shapes=[pltpu.VMEM((seq, d), F32), pltpu.VMEM((2, row_block, d), BF16),
                        pltpu.SemaphoreType.DMA(()), pltpu.SemaphoreType.DMA((3,))],
        compiler_params=_cparams(("arbitrary", "arbitrary")),
        name="moe_combine_layernorm",
    )(idx_be, y, x, g, b)


def _rope_tables(seq, rot_dim, split_halves=False):
    rows = seq // GRID_W
    row = jnp.repeat(jnp.arange(rows, dtype=jnp.int32), GRID_W).astype(F32)
    col = jnp.tile(jnp.arange(GRID_W, dtype=jnp.int32), rows).astype(F32)
    half = rot_dim // 2
    inv_freq = ROPE_THETA ** (-jnp.arange(0, half, 2, dtype=F32) / half)
    ang = jnp.concatenate([row[:, None] * inv_freq, col[:, None] * inv_freq], axis=-1)
    cos, sin = jnp.cos(ang), jnp.sin(ang)
    if split_halves:
        cos2 = jnp.concatenate([cos, cos], axis=1)
        sin2 = jnp.concatenate([-sin, sin], axis=1)
    else:
        cos2 = jnp.repeat(cos, 2, axis=1)
        sin2 = jnp.stack([-sin, sin], axis=-1).reshape(seq, rot_dim)
    pad = LANES - rot_dim
    if pad:
        cos2 = jnp.concatenate([cos2, jnp.ones((seq, pad), F32)], axis=1)
        sin2 = jnp.concatenate([sin2, jnp.zeros((seq, pad), F32)], axis=1)
    return cos2, sin2


def _moe(x1, xp1, logits_t, w1, w3, w2, ln_g, ln_b, layer, batch, seq):
    n_e = w1.shape[1]
    cap = CAPACITY_FACTOR * seq // n_e
    idx_col, gate_col = _route(logits_t, batch, seq, cap)
    idx_be = idx_col.reshape(batch * n_e, 1, cap)
    idx_eb = idx_col.reshape(batch, n_e, cap).transpose(1, 0, 2).reshape(n_e * batch, 1, cap)
    gate_eb = gate_col.reshape(batch, n_e, cap).transpose(1, 0, 2).reshape(n_e * batch * cap, 1)
    y = _ffn(idx_eb, xp1, w1, w3, w2, layer, gate_eb, batch, seq, cap, tf=256)
    return _combine(idx_be, y, x1, ln_g, ln_b, batch, seq, cap, n_e)


def _split_router(w_router):
    wr = w_router.T
    hi = wr.astype(BF16)
    lo = (wr - hi.astype(F32)).astype(BF16)
    return jnp.concatenate([hi, lo], axis=0)


def _even_mixer(xb, batch, seq, w_in, q_norm, w_uq, kv_norm, w_ukv, gate_w2, gate_b, gla_norm,
                cos_r, sin_r):
    d = w_in.shape[0]
    o_cq, o_ckv, o_kr = 0, MLA_Q_RANK, MLA_Q_RANK + MLA_KV_RANK
    o_gq = o_kr + MLA_ROPE_DIM
    o_gk = o_gq + GLA_HEADS * GLA_DK
    o_gv = o_gk + GLA_HEADS * GLA_DK
    o_gr = o_gv + GLA_HEADS * GLA_DV
    o_lat = o_gr + GLA_HEADS * GLA_DV
    o_end = o_lat + 2 * GLA_GATE_RANK
    zeros = lambda n: jnp.zeros((d, n), BF16)
    tail_pad = LANES - MLA_ROPE_DIM - 2 * GLA_GATE_RANK
    w_in = w_in.astype(BF16)
    w_in_l = jnp.concatenate([
        w_in[:, o_cq:o_ckv], w_in[:, o_gq:o_gk], w_in[:, o_gk:o_gv], w_in[:, o_ckv:o_kr],
        w_in[:, o_kr:o_gq], w_in[:, o_lat:o_end], zeros(tail_pad), zeros(H_GV - H_TAIL - LANES),
        w_in[:, o_gv:o_gr], w_in[:, o_gr:o_lat]], axis=1)
    h = _matmul(xb, w_in_l, F32, tm=1024, tn=1024)

    qk = MLA_NOPE_DIM + MLA_ROPE_DIM
    wuq_l = jnp.pad(w_uq.reshape(MLA_Q_RANK, MLA_HEADS, qk),
                    ((0, 0), (0, 0), (0, MLA_QK_PAD - qk))).reshape(MLA_Q_RANK, MLA_HEADS * MLA_QK_PAD)
    wukv3 = w_ukv.reshape(MLA_KV_RANK, MLA_HEADS, MLA_NOPE_DIM + MLA_V_DIM)
    wukv_l = jnp.concatenate([wukv3[:, :, :MLA_NOPE_DIM].reshape(MLA_KV_RANK, -1),
                              wukv3[:, :, MLA_NOPE_DIM:].reshape(MLA_KV_RANK, -1)], axis=1)
    gw = GLA_HEADS * GLA_DK
    wg = jnp.zeros((LANES, 2 * gw), F32)
    wg = wg.at[MLA_ROPE_DIM:MLA_ROPE_DIM + GLA_GATE_RANK, :gw].set(gate_w2[0])
    wg = wg.at[MLA_ROPE_DIM + GLA_GATE_RANK:MLA_ROPE_DIM + 2 * GLA_GATE_RANK, gw:].set(gate_w2[1])
    gb = jnp.concatenate([gate_b[0], gate_b[1]])[None, :]
    q_p, k_p, v_p, la = _mla_prep(h, q_norm[None, :], wuq_l.astype(BF16), kv_norm[None, :],
                                  wukv_l.astype(BF16), wg.astype(BF16), gb, cos_r, sin_r, seq, tm=256)
    o_mla = _attention(q_p, k_p, v_p, batch=batch, seq=seq, kv_heads=MLA_HEADS, group=1,
                       dqk=MLA_QK_PAD, dv=MLA_V_DIM, q_off=0, k_off=0, v_off=0, tq=2048)
    o_gla = _gla(h, la, gla_norm.reshape(GLA_HEADS, 1, GLA_DV), batch, seq)
    return o_mla, o_gla


def _odd_mixer(xb, batch, seq, w_qkv, q_norm, k_norm, cos_g, sin_g):
    hd = GQA_HEAD_DIM
    scale = hd ** -0.5 * LOG2_E
    n_qk, n_v = (GQA_HEADS + GQA_KV_HEADS) * hd, GQA_KV_HEADS * hd
    split = jnp.concatenate([jnp.arange(0, hd, 2), jnp.arange(1, hd, 2)])
    d_in = w_qkv.shape[0]
    w_qk = w_qkv[:, :n_qk].reshape(d_in, n_qk // hd, hd // 2, 2).transpose(0, 1, 3, 2).reshape(d_in, n_qk)
    w_l = jnp.concatenate([w_qk, w_qkv[:, n_qk:]], axis=1)
    gain_row = jnp.concatenate([jnp.tile(q_norm[split] * scale, GQA_HEADS), jnp.tile(k_norm[split], GQA_KV_HEADS),
                                jnp.ones((n_v,), F32)]).reshape(-1, 1, 4 * hd)
    normed_row = jnp.concatenate([jnp.ones((n_qk,), F32), jnp.zeros((n_v,), F32)]).reshape(-1, 1, 4 * hd)
    gains = jnp.concatenate([gain_row, normed_row], axis=1)
    qkv = _qkv_proj(xb, w_l.astype(BF16), gains, cos_g, sin_g, seq, tm=1024)
    group = GQA_HEADS // GQA_KV_HEADS
    return _attention(qkv, qkv, qkv, batch=batch, seq=seq, kv_heads=GQA_KV_HEADS, group=group,
                      dqk=hd, dv=hd, q_off=0, k_off=GQA_HEADS, v_off=GQA_HEADS + GQA_KV_HEADS, tq=512)


def kernel(x, mix_w_in, mla_q_norm, mla_w_uq, mla_kv_norm, mla_w_ukv, gla_gate_w2, gla_gate_b,
           gla_out_norm, mix_w_out, gqa_w_qkv, gqa_q_norm, gqa_k_norm, gqa_w_out,
           moe_router, moe_w1, moe_w3, moe_w2, ln_mix_g, ln_mix_b, ln_ffn_g, ln_ffn_b):
    batch, seq, d = x.shape
    cos_r, sin_r = _rope_tables(seq, MLA_ROPE_DIM)
    cos_g, sin_g = _rope_tables(seq, GQA_HEAD_DIM, split_halves=True)
    xf = x.reshape(batch * seq, d)
    xb = xf.astype(BF16)
    n_layers = moe_router.shape[0]
    for layer in range(n_layers):
        i = layer // 2
        wr_split = _split_router(moe_router[layer])
        g_mix, b_mix = ln_mix_g[layer][None, :], ln_mix_b[layer][None, :]
        if layer % 2 == 0:
            o_mla, o_gla = _even_mixer(xb, batch, seq, mix_w_in[i], mla_q_norm[i], mla_w_uq[i],
                                       mla_kv_norm[i], mla_w_ukv[i], gla_gate_w2[i], gla_gate_b[i],
                                       gla_out_norm[i], cos_r, sin_r)
            w_out = mix_w_out[i].astype(BF16)
            n_mla = MLA_HEADS * MLA_V_DIM
            x1, xp1, logits_t = _proj_ln([o_mla, o_gla], [w_out[:n_mla], w_out[n_mla:]], xf,
                                         g_mix, b_mix, wr_split, tm=512)
        else:
            o = _odd_mixer(xb, batch, seq, gqa_w_qkv[i], gqa_q_norm[i], gqa_k_norm[i], cos_g, sin_g)
            x1, xp1, logits_t = _proj_ln([o], [gqa_w_out[i].astype(BF16)], xf,
                                         g_mix, b_mix, wr_split, tm=512)
        xf, xb = _moe(x1, xp1, logits_t, moe_w1, moe_w3, moe_w2, ln_ffn_g[layer][None, :],
                      ln_ffn_b[layer][None, :], layer, batch, seq)
    return xf.reshape(batch, seq, d)
```
